```python
import math, functools
import jax, jax.numpy as jnp
from jax import lax
import numpy as np

D_MODEL = 1024
BATCH = 16
SEQ = 2048
DEPTH = 1
DEC_BATCH = 128
DEC_SEQ = 8
PAST_LEN = 8192
PAGE_SIZE = 128

MLA_HEADS = 8
MLA_Q_RANK = 256
MLA_KV_RANK = 128
MLA_NOPE = 64
MLA_ROPE = 32
MLA_V = 64
DSA_HEADS = 8
DSA_KV_HEADS = 2
DSA_HEAD_DIM = 64
IDX_HEADS = 8
IDX_DIM = 64
DSA_TOPK_MAX = 256
MIX_WIDTH = MLA_HEADS * MLA_V + DSA_HEADS * DSA_HEAD_DIM
IN_SPLITS = (MLA_Q_RANK, MLA_KV_RANK, MLA_ROPE, DSA_HEADS * DSA_HEAD_DIM, DSA_KV_HEADS * DSA_HEAD_DIM,
             DSA_KV_HEADS * DSA_HEAD_DIM, IDX_HEADS * IDX_DIM, IDX_HEADS, IDX_DIM)
IN_WIDTH = sum(IN_SPLITS)
N_MEM = 256
X_HEADS = 4
X_HEAD_DIM = 64
PEER_HEADS = 8
PEER_N_KEYS = 128
PEER_N_EXPERTS = PEER_N_KEYS * PEER_N_KEYS
PEER_KEY_DIM = 256
PEER_TOPK = 16
PEER_TOKEN_BLOCK = 256
ROPE_THETA = 10000.0
NORM_EPS = 1e-6
Q_BLOCK = 128

kernel_name = 'mla_dsa_peer_hybrid_step'


def rms_norm(x, g):
    xf = x.astype(jnp.float32)
    y = xf * lax.rsqrt(jnp.mean(xf * xf, axis=-1, keepdims=True) + NORM_EPS)
    return (y * g.astype(jnp.float32)).astype(x.dtype)


def rope(x, pos):
    half = x.shape[-1] // 2
    inv = ROPE_THETA ** (-jnp.arange(half, dtype=jnp.float32) / half)
    ang = pos.astype(jnp.float32)[:, None] * inv[None, :]
    ang = ang.reshape((pos.shape[0],) + (1,) * (x.ndim - 3) + (half,))
    cos, sin = jnp.cos(ang), jnp.sin(ang)
    xf = x.astype(jnp.float32)
    x1, x2 = xf[..., :half], xf[..., half:]
    return jnp.concatenate([x1 * cos - x2 * sin, x2 * cos + x1 * sin], axis=-1).astype(x.dtype)


def map_query_blocks(fn, *xs):
    t = xs[0].shape[1]
    qb = Q_BLOCK if t % Q_BLOCK == 0 else t
    nb = t // qb
    if nb == 1:
        return fn(*xs)
    blocks = tuple(x.reshape((x.shape[0], nb, qb) + x.shape[2:]).swapaxes(0, 1) for x in xs)
    out = lax.map(lambda bl: fn(*bl), blocks)
    return out.swapaxes(0, 1).reshape((out.shape[1], t) + out.shape[3:])


def gather_pages(pool, page_table):
    rows = pool[page_table]
    return rows.reshape((rows.shape[0], rows.shape[1] * rows.shape[2]) + rows.shape[3:])


def gather_rows(kv, idx):
    return jax.vmap(lambda a, i: a[i])(kv, idx)


def gather_paged_rows(pool, page_table, new, idx):
    flat = pool.reshape((-1,) + pool.shape[2:])
    past_len = page_table.shape[1] * pool.shape[1]
    def one(pt, nw, i):
        sp = jnp.minimum(i, past_len - 1)
        phys = pt[sp // PAGE_SIZE] * PAGE_SIZE + sp % PAGE_SIZE
        from_new = nw[jnp.clip(i - past_len, 0, nw.shape[0] - 1)]
        is_past = (i < past_len).reshape(i.shape + (1,) * (flat.ndim - 1))
        return jnp.where(is_past, flat[phys], from_new)
    return jax.vmap(one)(page_table, new, idx)


def mixer_inputs(xn, pos, w_in, mla_q_norm, w_uq, mla_kv_norm, w_uk):
    b, t, _ = xn.shape
    offs = [int(v) for v in np.cumsum(IN_SPLITS)[:-1]]
    cq, ckv, kr, dq, dk, dv, iq, iw, ik = jnp.split(xn @ w_in, offs, axis=-1)
    q = (rms_norm(cq, mla_q_norm) @ w_uq).reshape(b, t, MLA_HEADS, MLA_NOPE + MLA_ROPE)
    q_lat = jnp.einsum('bthn,chn->bthc', q[..., :MLA_NOPE], w_uk)
    q_r = rope(q[..., MLA_NOPE:], pos)
    ckv = rms_norm(ckv, mla_kv_norm)
    kr = rope(kr, pos)
    dq = rope(dq.reshape(b, t, DSA_HEADS, DSA_HEAD_DIM), pos)
    dk = rope(dk.reshape(b, t, DSA_KV_HEADS, DSA_HEAD_DIM), pos)
    kv = jnp.stack([dk, dv.reshape(b, t, DSA_KV_HEADS, DSA_HEAD_DIM)], axis=2)
    iq = rope(iq.reshape(b, t, IDX_HEADS, IDX_DIM), pos)
    ik = rope(ik, pos)
    return q_lat, q_r, ckv, kr, dq, kv, iq, iw, ik


def mla_attend(q_lat, q_r, q_pos, ckv, kr):
    scale = (MLA_NOPE + MLA_ROPE) ** -0.5
    k_pos = jnp.arange(ckv.shape[1])
    def block(ql, qr, qp):
        s = (jnp.einsum('bthc,bsc->bhts', ql, ckv).astype(jnp.float32)
             + jnp.einsum('bthr,bsr->bhts', qr, kr).astype(jnp.float32)) * scale
        causal = k_pos[None, :] <= qp[0][:, None]
        p = jax.nn.softmax(jnp.where(causal[None, None], s, -jnp.inf), axis=-1).astype(ckv.dtype)
        return jnp.einsum('bhts,bsc->bthc', p, ckv)
    return map_query_blocks(block, q_lat, q_r, q_pos)


def dsa_attend(q, iq, iw, q_pos, ik, topk, fetch_kv):
    k_pos = jnp.arange(ik.shape[1])
    group = DSA_HEADS // DSA_KV_HEADS
    def block(qb, iqb, iwb, qp):
        b, t = qb.shape[:2]
        dots = jnp.einsum('bthd,bsd->bths', iqb, ik).astype(jnp.float32)
        score = jnp.einsum('bth,bths->bts', iwb.astype(jnp.float32), jax.nn.relu(dots))
        causal = k_pos[None, :] <= qp[0][:, None]
        _, idx = lax.top_k(jnp.where(causal[None], score, -jnp.inf), topk)
        valid = idx <= qp[0][None, :, None]
        kv = fetch_kv(idx)
        kg, vg = kv[:, :, :, 0], kv[:, :, :, 1]
        qg = qb.reshape(b, t, DSA_KV_HEADS, group, DSA_HEAD_DIM)
        s = jnp.einsum('btgnd,btkgd->btgnk', qg, kg).astype(jnp.float32) * DSA_HEAD_DIM ** -0.5
        p = jax.nn.softmax(jnp.where(valid[:, :, None, None, :], s, -jnp.inf), axis=-1).astype(vg.dtype)
        return jnp.einsum('btgnk,btkgd->btgnd', p, vg).reshape(b, t, DSA_HEADS, DSA_HEAD_DIM)
    return map_query_blocks(block, q, iq, iw, q_pos)


def mixer_output(o_lat, o_dsa, w_uv, w_o):
    b, t = o_lat.shape[:2]
    o_mla = jnp.einsum('bthc,chv->bthv', o_lat, w_uv).reshape(b, t, MLA_HEADS * MLA_V)
    return jnp.concatenate([o_mla, o_dsa.reshape(b, t, DSA_HEADS * DSA_HEAD_DIM)], axis=-1) @ w_o


def memory_kv(mem, mem_norm, wk_x, wv_x):
    b, m, _ = mem.shape
    mn = rms_norm(mem, mem_norm)
    return ((mn @ wk_x).reshape(b, m, X_HEADS, X_HEAD_DIM), (mn @ wv_x).reshape(b, m, X_HEADS, X_HEAD_DIM))


def cross_attend(hn, k, v, wq_x, wo_x):
    b, t, _ = hn.shape
    q = (hn @ wq_x).reshape(b, t, X_HEADS, X_HEAD_DIM)
    s = jnp.einsum('bthd,bmhd->bhtm', q, k).astype(jnp.float32) * X_HEAD_DIM ** -0.5
    p = jax.nn.softmax(s, axis=-1).astype(v.dtype)
    return jnp.einsum('bhtm,bmhd->bthd', p, v).reshape(b, t, X_HEADS * X_HEAD_DIM) @ wo_x


def peer(xn, w_pq, peer_keys, peer_u, peer_v):
    b, t, d = xn.shape
    n = b * t
    blk = min(PEER_TOKEN_BLOCK, n)
    nb = -(-n // blk)
    flat = jnp.pad(xn.reshape(n, d), ((0, nb * blk - n), (0, 0)))
    def block(xb):
        q = (xb @ w_pq).reshape(blk, PEER_HEADS, 2, PEER_KEY_DIM // 2)
        s = jnp.einsum('nhpd,hpkd->nhpk', q, peer_keys).astype(jnp.float32)
        s_top, i_top = lax.top_k(s, PEER_TOPK)
        cand = (s_top[:, :, 0, :, None] + s_top[:, :, 1, None, :]).reshape(blk, PEER_HEADS, -1)
        cand_idx = (i_top[:, :, 0, :, None] * PEER_N_KEYS + i_top[:, :, 1, None, :]).reshape(blk, PEER_HEADS, -1)
        g_s, g_i = lax.top_k(cand, PEER_TOPK)
        e = jnp.take_along_axis(cand_idx, g_i, axis=-1)
        g = jax.nn.softmax(g_s, axis=-1)
        act = jax.nn.gelu(jnp.einsum('nd,nhkd->nhk', xb, peer_u[e]).astype(jnp.float32), approximate=False)
        return jnp.einsum('nhk,nhkd->nd', (g * act).astype(xb.dtype), peer_v[e])
    out = lax.map(block, flat.reshape(nb, blk, d))
    return out.reshape(nb * blk, d)[:n].reshape(b, t, d)


def setup_inputs(seed: int = 0) -> dict:
    key = jax.random.key(seed)
    ks = jax.random.split(key, 40)
    cnt = [0]
    def nxt():
        k = ks[cnt[0]]
        cnt[0] += 1
        return k
    def nrm(shape, scale=1.0):
        return jax.random.normal(nxt(), shape, jnp.float32) * scale
    def gain(shape):
        return 1.0 + 0.01 * nrm(shape)
    n_pages = PAST_LEN // PAGE_SIZE
    n_pool = (DEC_BATCH * n_pages * 5) // 4
    L, D = DEPTH, D_MODEL
    inp = {}
    inp['x_prompt'] = nrm((BATCH, SEQ, D))
    inp['x_sample'] = nrm((DEC_BATCH, DEC_SEQ, D))
    inp['cache_mla_ckv'] = nrm((L, n_pool, PAGE_SIZE, MLA_KV_RANK))
    inp['cache_mla_kr'] = nrm((L, n_pool, PAGE_SIZE, MLA_ROPE))
    inp['cache_dsa_kv'] = nrm((L, n_pool, PAGE_SIZE, 2, DSA_KV_HEADS, DSA_HEAD_DIM))
    inp['cache_dsa_idx'] = nrm((L, n_pool, PAGE_SIZE, IDX_DIM))
    inp['cache_mem_k'] = nrm((L, DEC_BATCH, N_MEM, X_HEADS, X_HEAD_DIM))
    inp['cache_mem_v'] = nrm((L, DEC_BATCH, N_MEM, X_HEADS, X_HEAD_DIM))
    inp['page_table'] = jax.random.permutation(nxt(), n_pool)[:DEC_BATCH * n_pages].reshape(DEC_BATCH, n_pages).astype(jnp.int32)
    inp['mem_prompt'] = nrm((BATCH, N_MEM, D))
    inp['norm_mix'] = gain((L, D))
    inp['w_in'] = nrm((L, D, IN_WIDTH), D ** -0.5)
    inp['mla_q_norm'] = gain((L, MLA_Q_RANK))
    inp['w_uq'] = nrm((L, MLA_Q_RANK, MLA_HEADS * (MLA_NOPE + MLA_ROPE)), MLA_Q_RANK ** -0.5)
    inp['mla_kv_norm'] = gain((L, MLA_KV_RANK))
    inp['w_uk'] = nrm((L, MLA_KV_RANK, MLA_HEADS, MLA_NOPE), MLA_KV_RANK ** -0.5)
    inp['w_uv'] = nrm((L, MLA_KV_RANK, MLA_HEADS, MLA_V), MLA_KV_RANK ** -0.5)
    inp['w_o'] = nrm((L, MIX_WIDTH, D), MIX_WIDTH ** -0.5)
    inp['norm_mem'] = gain((L, D))
    inp['mem_norm'] = gain((L, D))
    inp['wq_x'] = nrm((L, D, X_HEADS * X_HEAD_DIM), D ** -0.5)
    inp['wk_x'] = nrm((L, D, X_HEADS * X_HEAD_DIM), D ** -0.5)
    inp['wv_x'] = nrm((L, D, X_HEADS * X_HEAD_DIM), D ** -0.5)
    inp['wo_x'] = nrm((L, X_HEADS * X_HEAD_DIM, D), (X_HEADS * X_HEAD_DIM) ** -0.5)
    inp['norm_ffn'] = gain((L, D))
    inp['w_pq'] = nrm((L, D, PEER_HEADS * PEER_KEY_DIM), D ** -0.5)
    inp['peer_keys'] = nrm((L, PEER_HEADS, 2, PEER_N_KEYS, PEER_KEY_DIM // 2), (PEER_KEY_DIM // 2) ** -0.5)
    inp['peer_u'] = nrm((L, PEER_N_EXPERTS, D), D ** -0.5)
    inp['peer_v'] = nrm((L, PEER_N_EXPERTS, D), PEER_HEADS ** -0.5)
    inp['norm_final'] = gain((D,))
    return inp


def reference(x_prompt, x_sample, cache_mla_ckv, cache_mla_kr, cache_dsa_kv, cache_dsa_idx, cache_mem_k,
              cache_mem_v, page_table, mem_prompt, norm_mix, w_in, mla_q_norm, w_uq, mla_kv_norm, w_uk, w_uv,
              w_o, norm_mem, mem_norm, wq_x, wk_x, wv_x, wo_x, norm_ffn, w_pq, peer_keys, peer_u, peer_v,
              norm_final):
    k_prompt = min(DSA_TOPK_MAX, SEQ // 4)
    k_sample = min(DSA_TOPK_MAX, (PAST_LEN + DEC_SEQ) // 4)
    pos_p = jnp.arange(SEQ)
    pos_s = PAST_LEN + jnp.arange(DEC_SEQ)
    h_p, h_s = x_prompt, x_sample
    ckv_p_l, kr_p_l, kv_p_l, ik_p_l, mk_p_l, mv_p_l = [], [], [], [], [], []
    ckv_s_l, kr_s_l, kv_s_l, ik_s_l = [], [], [], []
    for l in range(DEPTH):
        xn = rms_norm(h_p, norm_mix[l])
        q_lat, q_r, ckv, kr, dq, kv, iq, iw, ik = mixer_inputs(xn, pos_p, w_in[l], mla_q_norm[l], w_uq[l],
                                                               mla_kv_norm[l], w_uk[l])
        o_lat = mla_attend(q_lat, q_r, pos_p[None], ckv, kr)
        o_dsa = dsa_attend(dq, iq, iw, pos_p[None], ik, k_prompt, functools.partial(gather_rows, kv))
        h_p = h_p + mixer_output(o_lat, o_dsa, w_uv[l], w_o[l])
        ckv_p_l.append(ckv); kr_p_l.append(kr); kv_p_l.append(kv); ik_p_l.append(ik)
        xn = rms_norm(h_s, norm_mix[l])
        q_lat, q_r, ckv_s, kr_s, dq, kv_s, iq, iw, ik_s = mixer_inputs(xn, pos_s, w_in[l], mla_q_norm[l], w_uq[l],
                                                                       mla_kv_norm[l], w_uk[l])
        ckv_all = jnp.concatenate([gather_pages(cache_mla_ckv[l], page_table), ckv_s], axis=1)
        kr_all = jnp.concatenate([gather_pages(cache_mla_kr[l], page_table), kr_s], axis=1)
        ik_all = jnp.concatenate([gather_pages(cache_dsa_idx[l], page_table), ik_s], axis=1)
        o_lat = mla_attend(q_lat, q_r, pos_s[None], ckv_all, kr_all)
        o_dsa = dsa_attend(dq, iq, iw, pos_s[None], ik_all, k_sample,
                           functools.partial(gather_paged_rows, cache_dsa_kv[l], page_table, kv_s))
        h_s = h_s + mixer_output(o_lat, o_dsa, w_uv[l], w_o[l])
        ckv_s_l.append(ckv_s); kr_s_l.append(kr_s); kv_s_l.append(kv_s); ik_s_l.append(ik_s)
        mk_p, mv_p = memory_kv(mem_prompt, mem_norm[l], wk_x[l], wv_x[l])
        mk_p_l.append(mk_p); mv_p_l.append(mv_p)
        h_p = h_p + cross_attend(rms_norm(h_p, norm_mem[l]), mk_p, mv_p, wq_x[l], wo_x[l])
        h_s = h_s + cross_attend(rms_norm(h_s, norm_mem[l]), cache_mem_k[l], cache_mem_v[l], wq_x[l], wo_x[l])
        h_p = h_p + peer(rms_norm(h_p, norm_ffn[l]), w_pq[l], peer_keys[l], peer_u[l], peer_v[l])
        h_s = h_s + peer(rms_norm(h_s, norm_ffn[l]), w_pq[l], peer_keys[l], peer_u[l], peer_v[l])
    y_prompt = rms_norm(h_p, norm_final)
    y_sample = rms_norm(h_s, norm_final)
    return (y_prompt, y_sample, jnp.stack(ckv_p_l), jnp.stack(kr_p_l), jnp.stack(kv_p_l), jnp.stack(ik_p_l),
            jnp.stack(mk_p_l), jnp.stack(mv_p_l), jnp.stack(ckv_s_l), jnp.stack(kr_s_l), jnp.stack(kv_s_l),
            jnp.stack(ik_s_l))
```

```python
import functools
import math

import numpy as np
import jax
import jax.numpy as jnp
from jax import lax
from jax.experimental import pallas as pl
from jax.experimental.pallas import tpu as pltpu

F32 = jnp.float32
BF16 = jnp.bfloat16
I32 = jnp.int32
U32 = jnp.uint32

NORM_EPS = 1e-6
ROPE_THETA = 10000.0
PAGE = 128
LANES = 128
SUBLANES = 8
VMEM_LIMIT = 56 * 1024 * 1024
NEG = -1e30
INT_MIN = -2 ** 31

MLA_HEADS, MLA_NOPE, MLA_ROPE, MLA_RANK = 8, 64, 32, 128
DSA_HEADS, DSA_KV_HEADS, DSA_DIM = 8, 2, 64
IDX_HEADS, IDX_DIM = 8, 64
DSA_TOPK = 256
X_HEADS, X_DIM = 4, 64
PEER_HEADS, PEER_KEYS, PEER_TOPK = 8, 128, 16

C_CQ, C_CKV, C_KR, C_DQ, C_DK, C_DV, C_IQ, C_IK, C_IW, C_END = 0, 256, 384, 512, 1024, 1152, 1280, 1792, 1920, 2048


def _cparams(sem):
    return pltpu.CompilerParams(dimension_semantics=sem, vmem_limit_bytes=VMEM_LIMIT)


def _rms(x, g):
    return x * lax.rsqrt(jnp.mean(x * x, axis=-1, keepdims=True) + NORM_EPS) * g


def _dot(a, b):
    return jnp.dot(a.astype(BF16), b.astype(BF16), preferred_element_type=F32)


def _dot_nt(a, b):
    return lax.dot_general(a.astype(BF16), b.astype(BF16), (((1,), (1,)), ((), ())), preferred_element_type=F32)


def _rope(x, cos, sin, d):
    w = x.shape[-1]
    half = d // 2
    lane = lax.broadcasted_iota(I32, x.shape, 1)
    first = (lane & (d - 1)) < half
    fwd = pltpu.roll(x, w - half, axis=1)
    bwd = pltpu.roll(x, half, axis=1)
    return x * cos + jnp.where(first, fwd, bwd) * sin


def _rope_tables(pos, d, rows):
    half = d // 2
    inv = ROPE_THETA ** (-jnp.arange(half, dtype=F32) / half)
    ang = pos.astype(F32)[:, None] * inv[None, :]
    cos, sin = jnp.cos(ang), jnp.sin(ang)
    cos = jnp.tile(jnp.concatenate([cos, cos], -1), (1, LANES // d))
    sin = jnp.tile(jnp.concatenate([-sin, sin], -1), (1, LANES // d))
    reps = max(1, rows // pos.shape[0])
    return jnp.tile(cos, (reps, 1)), jnp.tile(sin, (reps, 1))


def _mix_in_body(x_ref, g_ref, w_ref, qn_ref, wuq_ref, kvn_ref, wuk_ref, c64_ref, s64_ref, c32_ref, s32_ref,
                 ckv_ref, kr_ref, kv_ref, ik_ref, ql_ref, qr_ref, dq_ref, iq_ref, iw_ref):
    xn = _rms(x_ref[...], g_ref[...])
    p = _dot(xn, w_ref[...])
    c64, s64, c32, s32 = c64_ref[...], s64_ref[...], c32_ref[...], s32_ref[...]
    c64x4 = jnp.concatenate([c64] * 4, axis=1)
    s64x4 = jnp.concatenate([s64] * 4, axis=1)
    q = _dot(_rms(p[:, C_CQ:C_CKV], qn_ref[...]), wuq_ref[...])
    nope = MLA_HEADS * MLA_NOPE
    for h in range(MLA_HEADS):
        ql_ref[h] = _dot(q[:, h * MLA_NOPE:(h + 1) * MLA_NOPE], wuk_ref[h])
    qr_ref[...] = _rope(q[:, nope:], jnp.concatenate([c32] * 2, axis=1), jnp.concatenate([s32] * 2, axis=1), MLA_ROPE)
    ckv_ref[...] = _rms(p[:, C_CKV:C_KR], kvn_ref[...])
    kr_ref[...] = _rope(p[:, C_KR:C_DQ], c32, s32, MLA_ROPE)[:, :MLA_ROPE]
    dq_ref[...] = _rope(p[:, C_DQ:C_DK], c64x4, s64x4, DSA_DIM)
    kv_ref[...] = jnp.concatenate([_rope(p[:, C_DK:C_DV], c64, s64, DSA_DIM), p[:, C_DV:C_IQ]], axis=1)
    iq_ref[...] = _rope(p[:, C_IQ:C_IK], c64x4, s64x4, IDX_DIM)
    ik_ref[...] = _rope(p[:, C_IK:C_IW], c64, s64, IDX_DIM)[:, :IDX_DIM]
    iw_ref[...] = p[:, C_IW:C_IW + IDX_HEADS]


def _mixer_in(x, pos, tm, wts):
    n, d = x.shape
    t = pos.shape[0]
    c64, s64 = _rope_tables(pos, DSA_DIM, tm)
    c32, s32 = _rope_tables(pos, MLA_ROPE, tm)
    tb = c64.shape[0] // tm
    row = lambda i: (i, 0)
    tab = lambda i: (i % tb, 0)
    full2 = lambda i: (0, 0)
    full3 = lambda i: (0, 0, 0)
    out_shapes = [
        jax.ShapeDtypeStruct((n, MLA_RANK), F32), jax.ShapeDtypeStruct((n, MLA_ROPE), F32),
        jax.ShapeDtypeStruct((n, 2 * DSA_KV_HEADS * DSA_DIM), F32), jax.ShapeDtypeStruct((n, IDX_DIM), F32),
        jax.ShapeDtypeStruct((MLA_HEADS, n, MLA_RANK), F32), jax.ShapeDtypeStruct((n, MLA_HEADS * MLA_ROPE), F32),
        jax.ShapeDtypeStruct((n, DSA_HEADS * DSA_DIM), F32), jax.ShapeDtypeStruct((n, IDX_HEADS * IDX_DIM), F32),
        jax.ShapeDtypeStruct((n, IDX_HEADS), F32)]
    out_specs = [
        pl.BlockSpec((tm, MLA_RANK), row), pl.BlockSpec((tm, MLA_ROPE), row),
        pl.BlockSpec((tm, 2 * DSA_KV_HEADS * DSA_DIM), row), pl.BlockSpec((tm, IDX_DIM), row),
        pl.BlockSpec((MLA_HEADS, tm, MLA_RANK), lambda i: (0, i, 0)), pl.BlockSpec((tm, MLA_HEADS * MLA_ROPE), row),
        pl.BlockSpec((tm, DSA_HEADS * DSA_DIM), row), pl.BlockSpec((tm, IDX_HEADS * IDX_DIM), row),
        pl.BlockSpec((tm, IDX_HEADS), row)]
    in_specs = [
        pl.BlockSpec((tm, d), row), pl.BlockSpec((1, d), full2), pl.BlockSpec((d, C_END), full2),
        pl.BlockSpec((1, 256), full2), pl.BlockSpec((256, MLA_HEADS * (MLA_NOPE + MLA_ROPE)), full2),
        pl.BlockSpec((1, MLA_RANK), full2), pl.BlockSpec((MLA_HEADS, MLA_NOPE, MLA_RANK), full3),
        pl.BlockSpec((tm, LANES), tab), pl.BlockSpec((tm, LANES), tab),
        pl.BlockSpec((tm, LANES), tab), pl.BlockSpec((tm, LANES), tab)]
    return pl.pallas_call(
        _mix_in_body, grid=(n // tm,), in_specs=in_specs, out_specs=out_specs, out_shape=out_shapes,
        compiler_params=_cparams(("parallel",)), name="mixer_in",
    )(x, wts["norm_mix"], wts["w_in"], wts["q_norm"], wts["w_uq"], wts["kv_norm"], wts["w_ukt"], c64, s64, c32, s32)


def _online_update(m_ref, l_ref, acc_ref, s, mask, v):
    if mask is not None:
        s = jnp.where(mask, s, NEG)
    m_old = m_ref[...]
    m_new = jnp.maximum(m_old, jnp.max(s, axis=1, keepdims=True))
    p = jnp.exp(s - m_new)
    if mask is not None:
        p = jnp.where(mask, p, 0.0)
    alpha = jnp.exp(m_old - m_new)
    l_ref[...] = alpha * l_ref[...] + jnp.sum(p, axis=1, keepdims=True)
    acc_ref[...] = alpha * acc_ref[...] + _dot(p, v)
    m_ref[...] = m_new


def _init_state(m_ref, l_ref, acc_ref):
    m_ref[...] = jnp.full(m_ref.shape, NEG, F32)
    l_ref[...] = jnp.zeros(l_ref.shape, F32)
    acc_ref[...] = jnp.zeros(acc_ref.shape, F32)


def _stack_heads(x, n_heads, width):
    return jnp.concatenate([x[:, h * width:(h + 1) * width] for h in range(n_heads)], axis=0)


def _mla_prompt_body(ql_ref, qr_ref, ckv_ref, kr_ref, o_ref, m_ref, l_ref, acc_ref, *, tq, tk):
    i = pl.program_id(1)
    rows = MLA_HEADS * tq
    q_l = ql_ref[...].reshape(rows, MLA_RANK).astype(BF16)
    q_r = _stack_heads(qr_ref[...], MLA_HEADS, MLA_ROPE).astype(BF16)
    scale = (MLA_NOPE + MLA_ROPE) ** -0.5
    row_t = i * tq + (lax.broadcasted_iota(I32, (rows, 1), 0) & (tq - 1))
    _init_state(m_ref, l_ref, acc_ref)

    def step(j, carry):
        k0 = pl.multiple_of(j * tk, tk)
        kc = ckv_ref[pl.ds(k0, tk), :].astype(BF16)
        kr = kr_ref[pl.ds(k0, tk), :].astype(BF16)
        s = (_dot_nt(q_l, kc) + _dot_nt(q_r, kr)) * scale
        kpos = k0 + lax.broadcasted_iota(I32, (rows, tk), 1)
        _online_update(m_ref, l_ref, acc_ref, s, kpos <= row_t, kc)
        return carry

    lax.fori_loop(0, ((i + 1) * tq + tk - 1) // tk, step, 0)
    o_ref[...] = (acc_ref[...] / l_ref[...]).reshape(MLA_HEADS, tq, MLA_RANK)


def _mla_prompt(ql, qr, ckv, kr, b, t, tq, tk):
    n = b * t
    nq = t // tq
    rows = MLA_HEADS * tq
    return pl.pallas_call(
        functools.partial(_mla_prompt_body, tq=tq, tk=tk),
        grid=(b, nq),
        in_specs=[pl.BlockSpec((MLA_HEADS, tq, MLA_RANK), lambda bi, i: (0, bi * nq + i, 0)),
                  pl.BlockSpec((tq, MLA_HEADS * MLA_ROPE), lambda bi, i: (bi * nq + i, 0)),
                  pl.BlockSpec((t, MLA_RANK), lambda bi, i: (bi, 0)),
                  pl.BlockSpec((t, MLA_ROPE), lambda bi, i: (bi, 0))],
        out_specs=pl.BlockSpec((MLA_HEADS, tq, MLA_RANK), lambda bi, i: (0, bi * nq + i, 0)),
        out_shape=jax.ShapeDtypeStruct((MLA_HEADS, n, MLA_RANK), F32),
        scratch_shapes=[pltpu.VMEM((rows, 1), F32), pltpu.VMEM((rows, 1), F32), pltpu.VMEM((rows, MLA_RANK), F32)],
        compiler_params=_cparams(("parallel", "arbitrary")), name="mla_prompt",
    )(ql, qr, ckv, kr)


def _order_key(score):
    bits = lax.bitcast_convert_type(score + 0.0, I32)
    return bits ^ ((bits >> 31) & 0x7FFFFFFF)


def _kth_largest(count_ge, shape, k):
    kf = float(k)
    tau0 = jnp.where(count_ge(jnp.zeros(shape, I32)) >= kf, 0, INT_MIN).astype(I32)

    def bit(it, tau):
        cand = tau | lax.shift_left(jnp.int32(1), 30 - it)
        return jnp.where(count_ge(cand) >= kf, cand, tau)

    return lax.fori_loop(0, 31, bit, tau0)


def _select_mask(key, valid, tau, need, carry, tri):
    masks = []
    for c in range(key.shape[1] // LANES):
        kb = key[:, c * LANES:(c + 1) * LANES]
        eq, gt = kb == tau, kb > tau
        if valid is not None:
            vb = valid[:, c * LANES:(c + 1) * LANES]
            eq, gt = eq & vb, gt & vb
        eqf = jnp.where(eq, 1.0, 0.0)
        before = carry + jnp.dot(eqf.astype(BF16), tri, preferred_element_type=F32)
        masks.append(gt | (eq & (before < need)))
        carry = carry + jnp.sum(eqf, axis=1, keepdims=True)
    return jnp.concatenate(masks, axis=1), carry


def _strict_upper(n):
    r = lax.broadcasted_iota(I32, (n, n), 0)
    c = lax.broadcasted_iota(I32, (n, n), 1)
    return jnp.where(r < c, 1.0, 0.0).astype(BF16)


def _index_scores(iq, iw, ikc):
    score = None
    for h in range(IDX_HEADS):
        d = _dot_nt(iq[:, h * IDX_DIM:(h + 1) * IDX_DIM], ikc)
        term = iw[:, h:h + 1] * jnp.maximum(d, 0.0)
        score = term if score is None else score + term
    return score


def _dsa_prompt_body(iq_ref, iw_ref, ik_ref, dq_ref, kv_ref, o_ref, key_ref, m_ref, l_ref, acc_ref, *, tq, ck, topk):
    i = pl.program_id(1)
    nc = (i + 1) * (tq // ck)
    group = DSA_HEADS // DSA_KV_HEADS
    rows = group * tq
    iq = iq_ref[...].astype(BF16)
    iw = iw_ref[...]
    q_t = i * tq + lax.broadcasted_iota(I32, (tq, 1), 0)

    def causal(c):
        return (c * ck + lax.broadcasted_iota(I32, (tq, ck), 1)) <= q_t

    def score_chunk(c, carry):
        k0 = pl.multiple_of(c * ck, ck)
        score = _index_scores(iq, iw, ik_ref[pl.ds(k0, ck), :].astype(BF16))
        key_ref[c] = jnp.where(causal(c), _order_key(score), INT_MIN)
        return carry

    lax.fori_loop(0, nc, score_chunk, 0)

    def count(pred):
        def body(c, acc):
            return acc + jnp.sum(jnp.where(pred(key_ref[c]), 1.0, 0.0), axis=1, keepdims=True)
        return lax.fori_loop(0, nc, body, jnp.zeros((tq, 1), F32))

    tau = _kth_largest(lambda cand: count(lambda k: k >= cand), (tq, 1), topk)
    need = float(topk) - count(lambda k: k > tau)
    tri = _strict_upper(LANES)
    dq = dq_ref[...]
    qs = [_stack_heads(dq[:, g * group * DSA_DIM:(g + 1) * group * DSA_DIM], group, DSA_DIM).astype(BF16)
          for g in range(DSA_KV_HEADS)]
    for g in range(DSA_KV_HEADS):
        _init_state(m_ref.at[g], l_ref.at[g], acc_ref.at[g])
    scale = DSA_DIM ** -0.5

    def attend(c, carry):
        k0 = pl.multiple_of(c * ck, ck)
        mask, carry = _select_mask(key_ref[c], causal(c), tau, need, carry, tri)
        mask4 = jnp.concatenate([mask] * group, axis=0)
        kvc = kv_ref[pl.ds(k0, ck), :].astype(BF16)
        for g in range(DSA_KV_HEADS):
            kc = kvc[:, g * DSA_DIM:(g + 1) * DSA_DIM]
            vc = kvc[:, (DSA_KV_HEADS + g) * DSA_DIM:(DSA_KV_HEADS + g + 1) * DSA_DIM]
            _online_update(m_ref.at[g], l_ref.at[g], acc_ref.at[g], _dot_nt(qs[g], kc) * scale, mask4, vc)
        return carry

    lax.fori_loop(0, nc, attend, jnp.zeros((tq, 1), F32))
    outs = []
    for g in range(DSA_KV_HEADS):
        o = acc_ref[g] / l_ref[g]
        outs += [o[n * tq:(n + 1) * tq] for n in range(group)]
    o_ref[...] = jnp.concatenate(outs, axis=1)


def _dsa_prompt(iq, iw, ik, dq, kv, b, t, tq, ck, topk):
    n = b * t
    nq = t // tq
    group = DSA_HEADS // DSA_KV_HEADS
    rows = group * tq
    qrow = lambda bi, i: (bi * nq + i, 0)
    brow = lambda bi, i: (bi, 0)
    return pl.pallas_call(
        functools.partial(_dsa_prompt_body, tq=tq, ck=ck, topk=topk),
        grid=(b, nq),
        in_specs=[pl.BlockSpec((tq, IDX_HEADS * IDX_DIM), qrow), pl.BlockSpec((tq, IDX_HEADS), qrow),
                  pl.BlockSpec((t, IDX_DIM), brow), pl.BlockSpec((tq, DSA_HEADS * DSA_DIM), qrow),
                  pl.BlockSpec((t, 2 * DSA_KV_HEADS * DSA_DIM), brow)],
        out_specs=pl.BlockSpec((tq, DSA_HEADS * DSA_DIM), qrow),
        out_shape=jax.ShapeDtypeStruct((n, DSA_HEADS * DSA_DIM), F32),
        scratch_shapes=[pltpu.VMEM((t // ck, tq, ck), I32), pltpu.VMEM((DSA_KV_HEADS, rows, 1), F32),
                        pltpu.VMEM((DSA_KV_HEADS, rows, 1), F32), pltpu.VMEM((DSA_KV_HEADS, rows, DSA_DIM), F32)],
        compiler_params=_cparams(("parallel", "arbitrary")), name="dsa_prompt",
    )(iq, iw, ik, dq, kv)


def _mla_sample_body(pt_ref, ql_ref, qr_ref, ckvn_ref, krn_ref, *rest, t, g_pages):
    ckv_pages = rest[:g_pages]
    kr_pages = rest[g_pages:2 * g_pages]
    o_ref, m_ref, l_ref, acc_ref = rest[2 * g_pages:]
    j = pl.program_id(1)
    rows = MLA_HEADS * t
    q_l = ql_ref[...].reshape(rows, MLA_RANK).astype(BF16)
    q_r = _stack_heads(qr_ref[...], MLA_HEADS, MLA_ROPE).astype(BF16)
    scale = (MLA_NOPE + MLA_ROPE) ** -0.5

    @pl.when(j == 0)
    def _():
        _init_state(m_ref, l_ref, acc_ref)

    kc = jnp.concatenate([r[0] for r in ckv_pages], axis=0).astype(BF16)
    kr = jnp.concatenate([r[0] for r in kr_pages], axis=0).astype(BF16)
    _online_update(m_ref, l_ref, acc_ref, (_dot_nt(q_l, kc) + _dot_nt(q_r, kr)) * scale, None, kc)

    @pl.when(j == pl.num_programs(1) - 1)
    def _():
        kn = ckvn_ref[0].astype(BF16)
        s = (_dot_nt(q_l, kn) + _dot_nt(q_r, krn_ref[0].astype(BF16))) * scale
        row_t = lax.broadcasted_iota(I32, (rows, PAGE), 0) & (t - 1)
        _online_update(m_ref, l_ref, acc_ref, s, lax.broadcasted_iota(I32, (rows, PAGE), 1) <= row_t, kn)
        o_ref[...] = (acc_ref[...] / l_ref[...]).reshape(MLA_HEADS, t, MLA_RANK)


def _pad_new(x, b, t):
    return jnp.pad(x.reshape(b, t, x.shape[-1]), ((0, 0), (0, PAGE - t), (0, 0)))


def _mla_sample(page_table, ql, qr, ckv_new, kr_new, pool_ckv, pool_kr, b, t, g_pages):
    n_pages = page_table.shape[1]
    steps = n_pages // g_pages
    rows = MLA_HEADS * t
    page = lambda g: (lambda bi, j, pt: (pt[bi, j * g_pages + g], 0, 0))
    seq3 = lambda bi, j, pt: (bi, 0, 0)
    in_specs = [pl.BlockSpec((MLA_HEADS, t, MLA_RANK), lambda bi, j, pt: (0, bi, 0)),
                pl.BlockSpec((t, MLA_HEADS * MLA_ROPE), lambda bi, j, pt: (bi, 0)),
                pl.BlockSpec((1, PAGE, MLA_RANK), seq3), pl.BlockSpec((1, PAGE, MLA_ROPE), seq3)]
    in_specs += [pl.BlockSpec((1, PAGE, MLA_RANK), page(g)) for g in range(g_pages)]
    in_specs += [pl.BlockSpec((1, PAGE, MLA_ROPE), page(g)) for g in range(g_pages)]
    grid_spec = pltpu.PrefetchScalarGridSpec(
        num_scalar_prefetch=1, grid=(b, steps), in_specs=in_specs,
        out_specs=pl.BlockSpec((MLA_HEADS, t, MLA_RANK), lambda bi, j, pt: (0, bi, 0)),
        scratch_shapes=[pltpu.VMEM((rows, 1), F32), pltpu.VMEM((rows, 1), F32), pltpu.VMEM((rows, MLA_RANK), F32)])
    return pl.pallas_call(
        functools.partial(_mla_sample_body, t=t, g_pages=g_pages), grid_spec=grid_spec,
        out_shape=jax.ShapeDtypeStruct((MLA_HEADS, b * t, MLA_RANK), F32),
        compiler_params=_cparams(("parallel", "arbitrary")), name="mla_sample",
    )(page_table, ql, qr, _pad_new(ckv_new, b, t), _pad_new(kr_new, b, t),
      *([pool_ckv] * g_pages), *([pool_kr] * g_pages))


def _dsa_sample_body(pt_ref, iq_ref, iw_ref, dq_ref, ikn_ref, kvn_ref, *rest, t, g_pages, steps, topk):
    ik_pages = rest[:g_pages]
    kv_pages = rest[g_pages:2 * g_pages]
    o_ref, key_ref, keyn_ref, tau_ref, need_ref, carry_ref, m_ref, l_ref, acc_ref = rest[2 * g_pages:]
    j = pl.program_id(1)
    ck = g_pages * PAGE
    group = DSA_HEADS // DSA_KV_HEADS
    scale = DSA_DIM ** -0.5
    new_valid = lax.broadcasted_iota(I32, (t, PAGE), 1) <= lax.broadcasted_iota(I32, (t, PAGE), 0)

    @pl.when(j < steps)
    def _():
        iq = iq_ref[...].astype(BF16)
        ikc = jnp.concatenate([r[0] for r in ik_pages], axis=0).astype(BF16)
        key_ref[j] = _order_key(_index_scores(iq, iw_ref[...], ikc))

        @pl.when(j == steps - 1)
        def _():
            sn = _index_scores(iq, iw_ref[...], ikn_ref[0].astype(BF16))
            keyn_ref[...] = jnp.where(new_valid, _order_key(sn), INT_MIN)

    @pl.when(j == steps)
    def _():
        def count(pred):
            def body(c, acc):
                return acc + jnp.sum(jnp.where(pred(key_ref[c]), 1.0, 0.0), axis=1, keepdims=True)
            past = lax.fori_loop(0, steps, body, jnp.zeros((t, 1), F32))
            return past + jnp.sum(jnp.where(pred(keyn_ref[...]), 1.0, 0.0), axis=1, keepdims=True)

        tau = _kth_largest(lambda cand: count(lambda k: k >= cand), (t, 1), topk)
        tau_ref[...] = tau
        need_ref[...] = float(topk) - count(lambda k: k > tau)
        carry_ref[...] = jnp.zeros((t, 1), F32)
        for g in range(DSA_KV_HEADS):
            _init_state(m_ref.at[g], l_ref.at[g], acc_ref.at[g])

    def attend(key, valid, kvc, tri):
        mask, carry = _select_mask(key, valid, tau_ref[...], need_ref[...], carry_ref[...], tri)
        carry_ref[...] = carry
        mask4 = jnp.concatenate([mask] * group, axis=0)
        dq = dq_ref[...]
        for g in range(DSA_KV_HEADS):
            qs = _stack_heads(dq[:, g * group * DSA_DIM:(g + 1) * group * DSA_DIM], group, DSA_DIM)
            kc = kvc[:, g * DSA_DIM:(g + 1) * DSA_DIM]
            vc = kvc[:, (DSA_KV_HEADS + g) * DSA_DIM:(DSA_KV_HEADS + g + 1) * DSA_DIM]
            _online_update(m_ref.at[g], l_ref.at[g], acc_ref.at[g], _dot_nt(qs, kc) * scale, mask4, vc)

    @pl.when(j >= steps)
    def _():
        kvc = jnp.concatenate([r[0] for r in kv_pages], axis=0).astype(BF16)
        attend(key_ref[j - steps], None, kvc, _strict_upper(LANES))

        @pl.when(j == 2 * steps - 1)
        def _():
            attend(keyn_ref[...], new_valid, kvn_ref[0].astype(BF16), _strict_upper(LANES))
            outs = []
            for g in range(DSA_KV_HEADS):
                o = acc_ref[g] / l_ref[g]
                outs += [o[n * t:(n + 1) * t] for n in range(group)]
            o_ref[...] = jnp.concatenate(outs, axis=1)


def _dsa_sample(page_table, iq, iw, dq, ik_new, kv_new, pool_ik, pool_kv, b, t, g_pages, topk):
    n_pages = page_table.shape[1]
    steps = n_pages // g_pages
    group = DSA_HEADS // DSA_KV_HEADS
    rows = group * t
    kvw = 2 * DSA_KV_HEADS * DSA_DIM
    ik_page = lambda g: (lambda bi, j, pt: (pt[bi, jnp.minimum(j, steps - 1) * g_pages + g], 0, 0))
    kv_page = lambda g: (lambda bi, j, pt: (pt[bi, jnp.maximum(j - steps, 0) * g_pages + g], 0, 0))
    seq2 = lambda bi, j, pt: (bi, 0)
    seq3 = lambda bi, j, pt: (bi, 0, 0)
    in_specs = [pl.BlockSpec((t, IDX_HEADS * IDX_DIM), seq2), pl.BlockSpec((t, IDX_HEADS), seq2),
                pl.BlockSpec((t, DSA_HEADS * DSA_DIM), seq2),
                pl.BlockSpec((1, PAGE, IDX_DIM), seq3), pl.BlockSpec((1, PAGE, kvw), seq3)]
    in_specs += [pl.BlockSpec((1, PAGE, IDX_DIM), ik_page(g)) for g in range(g_pages)]
    in_specs += [pl.BlockSpec((1, PAGE, kvw), kv_page(g)) for g in range(g_pages)]
    grid_spec = pltpu.PrefetchScalarGridSpec(
        num_scalar_prefetch=1, grid=(b, 2 * steps), in_specs=in_specs,
        out_specs=pl.BlockSpec((t, DSA_HEADS * DSA_DIM), seq2),
        scratch_shapes=[pltpu.VMEM((steps, t, g_pages * PAGE), I32), pltpu.VMEM((t, PAGE), I32),
                        pltpu.VMEM((t, 1), I32), pltpu.VMEM((t, 1), F32), pltpu.VMEM((t, 1), F32),
                        pltpu.VMEM((DSA_KV_HEADS, rows, 1), F32), pltpu.VMEM((DSA_KV_HEADS, rows, 1), F32),
                        pltpu.VMEM((DSA_KV_HEADS, rows, DSA_DIM), F32)])
    return pl.pallas_call(
        functools.partial(_dsa_sample_body, t=t, g_pages=g_pages, steps=steps, topk=topk), grid_spec=grid_spec,
        out_shape=jax.ShapeDtypeStruct((b * t, DSA_HEADS * DSA_DIM), F32),
        compiler_params=_cparams(("parallel", "arbitrary")), name="dsa_sample",
    )(page_table, iq, iw, dq, _pad_new(ik_new, b, t), _pad_new(kv_new, b, t),
      *([pool_ik] * g_pages), *([pool_kv] * g_pages))


def _mix_out_body(ol_ref, od_ref, h_ref, wuv_ref, wo_ref, o_ref):
    mla = jnp.concatenate([_dot(ol_ref[h], wuv_ref[h]) for h in range(MLA_HEADS)], axis=1)
    half = mla.shape[1]
    o_ref[...] = h_ref[...] + _dot(mla, wo_ref[:half, :]) + _dot(od_ref[...], wo_ref[half:, :])


def _mixer_out(o_lat, o_dsa, h, wts, tm):
    n, d = h.shape
    row = lambda i: (i, 0)
    v_dim = wts["w_uvh"].shape[-1]
    mix = wts["w_o"].shape[0]
    return pl.pallas_call(
        _mix_out_body, grid=(n // tm,),
        in_specs=[pl.BlockSpec((MLA_HEADS, tm, MLA_RANK), lambda i: (0, i, 0)), pl.BlockSpec((tm, DSA_HEADS * DSA_DIM), row),
                  pl.BlockSpec((tm, d), row), pl.BlockSpec((MLA_HEADS, MLA_RANK, v_dim), lambda i: (0, 0, 0)),
                  pl.BlockSpec((mix, d), lambda i: (0, 0))],
        out_specs=pl.BlockSpec((tm, d), row), out_shape=jax.ShapeDtypeStruct((n, d), F32),
        compiler_params=_cparams(("parallel",)), name="mixer_out",
    )(o_lat, o_dsa, h, wts["w_uvh"], wts["w_o"])


def _mem_kv_body(m_ref, g_ref, w_ref, k_ref, v_ref):
    p = _dot(_rms(m_ref[...], g_ref[...]), w_ref[...])
    half = p.shape[1] // 2
    k_ref[...] = p[:, :half]
    v_ref[...] = p[:, half:]


def _mem_kv(mem, wts, tm):
    n, d = mem.shape
    w = wts["w_kvx"]
    half = w.shape[1] // 2
    row = lambda i: (i, 0)
    return pl.pallas_call(
        _mem_kv_body, grid=(n // tm,),
        in_specs=[pl.BlockSpec((tm, d), row), pl.BlockSpec((1, d), lambda i: (0, 0)), pl.BlockSpec(w.shape, lambda i: (0, 0))],
        out_specs=[pl.BlockSpec((tm, half), row)] * 2, out_shape=[jax.ShapeDtypeStruct((n, half), F32)] * 2,
        compiler_params=_cparams(("parallel",)), name="mem_kv",
    )(mem, wts["mem_norm"], w)


def _cross_body(h_ref, g_ref, wq_ref, k_ref, v_ref, wo_ref, o_ref):
    h = h_ref[...]
    q = _dot(_rms(h, g_ref[...]), wq_ref[...])
    k = k_ref[0].astype(BF16)
    v = v_ref[0].astype(BF16)
    outs = []
    for hd in range(X_HEADS):
        sl = slice(hd * X_DIM, (hd + 1) * X_DIM)
        s = _dot_nt(q[:, sl], k[:, sl]) * X_DIM ** -0.5
        e = jnp.exp(s - jnp.max(s, axis=1, keepdims=True))
        p = e / jnp.sum(e, axis=1, keepdims=True)
        outs.append(_dot(p, v[:, sl]))
    o_ref[...] = h + _dot(jnp.concatenate(outs, axis=1), wo_ref[...])


def _cross(h, mk, mv, wts, b, tm):
    n, d = h.shape
    per = (n // b) // tm
    m, w = mk.shape[1], mk.shape[2]
    row = lambda i: (i, 0)
    mem = lambda i: (i // per, 0, 0)
    return pl.pallas_call(
        _cross_body, grid=(n // tm,),
        in_specs=[pl.BlockSpec((tm, d), row), pl.BlockSpec((1, d), lambda i: (0, 0)), pl.BlockSpec((d, w), lambda i: (0, 0)),
                  pl.BlockSpec((1, m, w), mem), pl.BlockSpec((1, m, w), mem), pl.BlockSpec((w, d), lambda i: (0, 0))],
        out_specs=pl.BlockSpec((tm, d), row), out_shape=jax.ShapeDtypeStruct((n, d), F32),
        compiler_params=_cparams(("parallel",)), name="cross_attn",
    )(h, wts["norm_mem"], wts["wq_x"], mk, mv, wts["wo_x"])


def _extract_topk(s, payload, k):
    r = s.shape[0]
    row = lax.broadcasted_iota(I32, s.shape, 0)
    vals, pays = [], []
    for _ in range(k):
        m = jnp.max(s, axis=0, keepdims=True)
        first = jnp.min(jnp.where(s == m, row, r), axis=0, keepdims=True)
        hit = row == first
        vals.append(m)
        pays.append(first if payload is None else jnp.sum(jnp.where(hit, payload, 0), axis=0, keepdims=True))
        s = jnp.where(hit, -jnp.inf, s)
    return jnp.concatenate(vals, axis=0), jnp.concatenate(pays, axis=0)


def _pair_blocks(k):
    blocks = []
    a = 0
    while a < k and k // (a + 1) > 1:
        nb = k // (a + 1)
        blocks.append((a, nb, -(-nb // SUBLANES) * SUBLANES))
        a += 1
    return blocks, a


def _peer_route_body(h_ref, g_ref, wpq_ref, keys_ref, xn_ref, e_ref, gate_ref, q_scr, tv_scr, ti_scr, gs_scr, ge_scr):
    tm = h_ref.shape[0]
    k = PEER_TOPK
    xn = _rms(h_ref[...], g_ref[...])
    xn_ref[...] = xn
    q = _dot(xn, wpq_ref[...])
    n_sub = 2 * PEER_HEADS
    kd = q.shape[1] // n_sub
    for hp in range(n_sub):
        q_scr[hp] = q[:, hp * kd:(hp + 1) * kd].astype(BF16)

    def level1(hp, carry):
        s = _dot_nt(keys_ref[hp], q_scr[hp])
        tv_scr[hp], ti_scr[hp] = _extract_topk(s, None, k)
        return carry

    lax.fori_loop(0, n_sub, level1, 0)
    blocks, a_tail = _pair_blocks(k)

    def level2(h, carry):
        t1, i1 = tv_scr[2 * h], ti_scr[2 * h]
        t2, i2 = tv_scr[2 * h + 1], ti_scr[2 * h + 1]
        cand, ids = [], []
        for a, nb, rows in blocks:
            live = lax.broadcasted_iota(I32, (rows, tm), 0) < nb
            cand.append(jnp.where(live, t1[a:a + 1] + t2[:rows], -jnp.inf))
            ids.append(i1[a:a + 1] * PEER_KEYS + i2[:rows])
        cand.append(t1[a_tail:] + t2[0:1])
        ids.append(i1[a_tail:] * PEER_KEYS + i2[0:1])
        gs_scr[h], ge_scr[h] = _extract_topk(jnp.concatenate(cand, axis=0), jnp.concatenate(ids, axis=0), k)
        return carry

    lax.fori_loop(0, PEER_HEADS, level2, 0)
    gs = gs_scr[...]
    e = jnp.exp(gs - jnp.max(gs, axis=1, keepdims=True))
    gate = e / jnp.sum(e, axis=1, keepdims=True)
    gate_ref[...] = gate.reshape(PEER_HEADS * k, tm).T
    e_ref[...] = ge_scr[...].reshape(PEER_HEADS * k, tm).T


def _peer_route(h, wts, tm):
    n, d = h.shape
    slots = PEER_HEADS * PEER_TOPK
    keys = wts["peer_keys"]
    n_sub, n_keys, kd = keys.shape
    row = lambda i: (i, 0)
    return pl.pallas_call(
        _peer_route_body, grid=(n // tm,),
        in_specs=[pl.BlockSpec((tm, d), row), pl.BlockSpec((1, d), lambda i: (0, 0)),
                  pl.BlockSpec((d, n_sub * kd), lambda i: (0, 0)), pl.BlockSpec(keys.shape, lambda i: (0, 0, 0))],
        out_specs=[pl.BlockSpec((tm, d), row), pl.BlockSpec((tm, slots), row), pl.BlockSpec((tm, slots), row)],
        out_shape=[jax.ShapeDtypeStruct((n, d), F32), jax.ShapeDtypeStruct((n, slots), I32),
                   jax.ShapeDtypeStruct((n, slots), F32)],
        scratch_shapes=[pltpu.VMEM((n_sub, tm, kd), BF16), pltpu.VMEM((n_sub, PEER_TOPK, tm), F32),
                        pltpu.VMEM((n_sub, PEER_TOPK, tm), I32), pltpu.VMEM((PEER_HEADS, PEER_TOPK, tm), F32),
                        pltpu.VMEM((PEER_HEADS, PEER_TOPK, tm), I32)],
        compiler_params=_cparams(("parallel",)), name="peer_route",
    )(h, wts["norm_ffn"], wts["w_pq"], keys)


def _pack_table(w):
    e, d = w.shape
    bits = lax.bitcast_convert_type(w.astype(jnp.bfloat16), jnp.uint16).astype(U32)
    packed = bits[:, :d // 2] | (bits[:, d // 2:] << 16)
    return packed.reshape(e * d // (2 * LANES), LANES)


def _unpack_row(tbl_ref, e, rows):
    w = tbl_ref[pl.ds(pl.multiple_of(e * rows, rows), rows), :]
    lo = lax.bitcast_convert_type(w << 16, F32)
    hi = lax.bitcast_convert_type(w & jnp.uint32(0xFFFF0000), F32)
    return lo, hi


def _load_table(tbl_hbm, tbl_ref, sem):
    @pl.when(pl.program_id(0) == 0)
    def _():
        cp = pltpu.make_async_copy(tbl_hbm, tbl_ref, sem)
        cp.start()
        cp.wait()


def _peer_up_body(e_ref, x_ref, gate_ref, tbl_hbm, a_ref, tbl_ref, p_ref, sem, *, tb, slots, rows):
    _load_table(tbl_hbm, tbl_ref, sem)

    def token(n, carry):
        xv = x_ref[pl.ds(pl.multiple_of(n * 2 * rows, 2 * rows), 2 * rows), :]
        xa, xb = xv[:rows], xv[rows:]
        for k in range(slots):
            lo, hi = _unpack_row(tbl_ref, e_ref[n, k], rows)
            p_ref[pl.ds(k * rows, rows), :] = lo * xa + hi * xb
        t = p_ref[pl.ds(0, slots, stride=rows), :]
        for r in range(1, rows):
            t = t + p_ref[pl.ds(r, slots, stride=rows), :]
        hrow = jnp.sum(t.T, axis=0, keepdims=True)
        act = 0.5 * hrow * (1.0 + lax.erf(hrow * (2.0 ** -0.5)))
        a_ref[n] = gate_ref[n] * act
        return carry

    lax.fori_loop(0, tb, token, 0)


def _peer_down_body(e_ref, a_ref, h_ref, g_ref, tbl_hbm, y_ref, tbl_ref, sem, *, tb, slots, rows):
    _load_table(tbl_hbm, tbl_ref, sem)
    d = 2 * rows * LANES

    def token(n, carry):
        acc = [jnp.zeros((rows, LANES), F32) for _ in range(4)]
        for k in range(slots):
            lo, hi = _unpack_row(tbl_ref, e_ref[n, k], rows)
            a = a_ref[n, k]
            acc[2 * (k % 2)] = acc[2 * (k % 2)] + a * lo
            acc[2 * (k % 2) + 1] = acc[2 * (k % 2) + 1] + a * hi
        r0 = pl.multiple_of(n * 2 * rows, 2 * rows)
        h = h_ref[pl.ds(r0, 2 * rows), :] + jnp.concatenate([acc[0] + acc[2], acc[1] + acc[3]], axis=0)
        ss = jnp.sum(jnp.sum(h * h, axis=1, keepdims=True), axis=0, keepdims=True)
        y_ref[pl.ds(r0, 2 * rows), :] = h * lax.rsqrt(ss / d + NORM_EPS) * g_ref[...]
        return carry

    lax.fori_loop(0, tb, token, 0)


def _peer_experts(xn, h, e, gate, wts, tb):
    n, d = h.shape
    slots = e.shape[1]
    rows = d // (2 * LANES)
    vr = 2 * rows
    tok = lambda i: (i, 0)
    smem = functools.partial(pl.BlockSpec, memory_space=pltpu.SMEM)
    tbl_shape = wts["peer_u"].shape
    common = dict(grid=(n // tb,), compiler_params=_cparams(("arbitrary",)))
    tok3 = lambda i: (i, 0, 0)
    a = pl.pallas_call(
        functools.partial(_peer_up_body, tb=tb, slots=slots, rows=rows),
        in_specs=[smem((tb, slots), tok), pl.BlockSpec((tb * vr, LANES), tok), pl.BlockSpec((tb, 1, slots), tok3),
                  pl.BlockSpec(memory_space=pl.ANY)],
        out_specs=pl.BlockSpec((tb, 1, slots), tok3), out_shape=jax.ShapeDtypeStruct((n, 1, slots), F32),
        scratch_shapes=[pltpu.VMEM(tbl_shape, U32), pltpu.VMEM((slots * rows, LANES), F32), pltpu.SemaphoreType.DMA(())],
        name="peer_up", **common,
    )(e, xn.reshape(n * vr, LANES), gate.reshape(n, 1, slots), wts["peer_u"]).reshape(n, slots)
    y = pl.pallas_call(
        functools.partial(_peer_down_body, tb=tb, slots=slots, rows=rows),
        in_specs=[smem((tb, slots), tok), smem((tb, slots), tok), pl.BlockSpec((tb * vr, LANES), tok),
                  pl.BlockSpec((vr, LANES), lambda i: (0, 0)), pl.BlockSpec(memory_space=pl.ANY)],
        out_specs=pl.BlockSpec((tb * vr, LANES), tok), out_shape=jax.ShapeDtypeStruct((n * vr, LANES), F32),
        scratch_shapes=[pltpu.VMEM(tbl_shape, U32), pltpu.SemaphoreType.DMA(())],
        name="peer_down", **common,
    )(e, a, h.reshape(n * vr, LANES), wts["norm_final"].reshape(vr, LANES), wts["peer_v"])
    return y.reshape(n, d)


def _prep_weights(norm_mix, w_in, mla_q_norm, w_uq, mla_kv_norm, w_uk, w_uv, w_o, norm_mem, mem_norm, wq_x, wk_x,
                  wv_x, wo_x, norm_ffn, w_pq, peer_keys, peer_u, peer_v, norm_final):
    d = w_in.shape[0]
    widths = (256, 128, 32, 512, 128, 128, 512, 8, 64)
    offs = np.concatenate([[0], np.cumsum(widths)])
    cq, ckv, kr, dq, dk, dv, iq, iw, ik = [w_in[:, offs[i]:offs[i + 1]] for i in range(9)]
    pad = lambda w, to: jnp.pad(w, ((0, 0), (0, to - w.shape[1])))
    fused = jnp.concatenate([cq, ckv, pad(kr, 128), dq, dk, dv, iq, pad(ik, 128), pad(iw, 128)], axis=1)
    hd = MLA_NOPE + MLA_ROPE
    uq = w_uq.reshape(w_uq.shape[0], MLA_HEADS, hd)
    uq = jnp.concatenate([uq[:, :, :MLA_NOPE].reshape(-1, MLA_HEADS * MLA_NOPE),
                          uq[:, :, MLA_NOPE:].reshape(-1, MLA_HEADS * MLA_ROPE)], axis=1)
    row = lambda v: v.reshape(1, -1)
    n_sub = peer_keys.shape[0] * peer_keys.shape[1]
    return dict(
        norm_mix=row(norm_mix), w_in=fused.astype(BF16), q_norm=row(mla_q_norm), w_uq=uq.astype(BF16),
        kv_norm=row(mla_kv_norm), w_ukt=jnp.transpose(w_uk, (1, 2, 0)).astype(BF16),
        w_uvh=jnp.transpose(w_uv, (1, 0, 2)).astype(BF16), w_o=w_o.astype(BF16),
        norm_mem=row(norm_mem), mem_norm=row(mem_norm), wq_x=wq_x.astype(BF16),
        w_kvx=jnp.concatenate([wk_x, wv_x], axis=1).astype(BF16), wo_x=wo_x.astype(BF16),
        norm_ffn=row(norm_ffn), w_pq=w_pq.astype(BF16),
        peer_keys=peer_keys.reshape(n_sub, peer_keys.shape[2], peer_keys.shape[3]).astype(BF16),
        peer_u=_pack_table(peer_u), peer_v=_pack_table(peer_v), norm_final=norm_final)


def _tile(n, want):
    t = min(n, want)
    while n % t:
        t //= 2
    return t


def _group_step(x, pos, b, wts, attend, mk, mv):
    n, d = x.shape
    t = n // b
    ckv, kr, kv, ik, ql, qr, dq, iq, iw = _mixer_in(x, pos, _tile(n, 512), wts)
    o_lat, o_dsa = attend(ql, qr, ckv, kr, dq, kv, iq, iw, ik)
    h = _mixer_out(o_lat, o_dsa, x, wts, _tile(n, 512))
    h = _cross(h, mk, mv, wts, b, _tile(t, 512))
    xn, e, gate = _peer_route(h, wts, _tile(n, 128))
    y = _peer_experts(xn, h, e, gate, wts, _tile(n, 128))
    return y, (ckv, kr, kv, ik)


def kernel(x_prompt, x_sample, cache_mla_ckv, cache_mla_kr, cache_dsa_kv, cache_dsa_idx, cache_mem_k, cache_mem_v, page_table, mem_prompt, norm_mix, w_in, mla_q_norm, w_uq, mla_kv_norm, w_uk, w_uv, w_o, norm_mem, mem_norm, wq_x, wk_x, wv_x, wo_x, norm_ffn, w_pq, peer_keys, peer_u, peer_v, norm_final):
    depth = w_in.shape[0]
    assert depth == 1, "one decoder layer"
    bp, tp, d = x_prompt.shape
    bs, ts, _ = x_sample.shape
    n_pool, page = cache_mla_ckv.shape[1], cache_mla_ckv.shape[2]
    assert page == PAGE
    past = page_table.shape[1] * page
    wts = _prep_weights(norm_mix[0], w_in[0], mla_q_norm[0], w_uq[0], mla_kv_norm[0], w_uk[0], w_uv[0], w_o[0],
                        norm_mem[0], mem_norm[0], wq_x[0], wk_x[0], wv_x[0], wo_x[0], norm_ffn[0], w_pq[0],
                        peer_keys[0], peer_u[0], peer_v[0], norm_final)
    k_prompt = min(DSA_TOPK, tp // 4)
    k_sample = min(DSA_TOPK, (past + ts) // 4)
    xw = X_HEADS * X_DIM
    n_mem = mem_prompt.shape[1]

    def attend_prompt(ql, qr, ckv, kr, dq, kv, iq, iw, ik):
        o_lat = _mla_prompt(ql, qr, ckv, kr, bp, tp, _tile(tp, 128), _tile(tp, 512))
        o_dsa = _dsa_prompt(iq, iw, ik, dq, kv, bp, tp, _tile(tp, 256), _tile(tp, 256), k_prompt)
        return o_lat, o_dsa

    g_pages = _tile(page_table.shape[1], 8)

    def attend_sample(ql, qr, ckv, kr, dq, kv, iq, iw, ik):
        o_lat = _mla_sample(page_table, ql, qr, ckv, kr, cache_mla_ckv[0], cache_mla_kr[0], bs, ts, g_pages)
        o_dsa = _dsa_sample(page_table, iq, iw, dq, ik, kv, cache_dsa_idx[0],
                            cache_dsa_kv[0].reshape(n_pool, page, -1), bs, ts, g_pages, k_sample)
        return o_lat, o_dsa

    mk_p, mv_p = _mem_kv(mem_prompt.reshape(bp * n_mem, d), wts, _tile(bp * n_mem, 512))
    y_p, (ckv_p, kr_p, kv_p, ik_p) = _group_step(
        x_prompt.reshape(bp * tp, d), jnp.arange(tp), bp, wts, attend_prompt,
        mk_p.reshape(bp, n_mem, xw), mv_p.reshape(bp, n_mem, xw))
    y_s, (ckv_s, kr_s, kv_s, ik_s) = _group_step(
        x_sample.reshape(bs * ts, d), past + jnp.arange(ts), bs, wts, attend_sample,
        cache_mem_k[0].reshape(bs, n_mem, xw), cache_mem_v[0].reshape(bs, n_mem, xw))
    kv_shape = (2, DSA_KV_HEADS, DSA_DIM)
    return (y_p.reshape(bp, tp, d), y_s.reshape(bs, ts, d),
            ckv_p.reshape(1, bp, tp, -1), kr_p.reshape(1, bp, tp, -1), kv_p.reshape((1, bp, tp) + kv_shape),
            ik_p.reshape(1, bp, tp, -1), mk_p.reshape(1, bp, n_mem, X_HEADS, X_DIM),
            mv_p.reshape(1, bp, n_mem, X_HEADS, X_DIM),
            ckv_s.reshape(1, bs, ts, -1), kr_s.reshape(1, bs, ts, -1), kv_s.reshape((1, bs, ts) + kv_shape),
            ik_s.reshape(1, bs, ts, -1))
```

```python
import functools
import math

import numpy as np
import jax
import jax.numpy as jnp
from jax import lax
from jax.experimental import pallas as pl
from jax.experimental.pallas import tpu as pltpu

F32 = jnp.float32
BF16 = jnp.bfloat16
I32 = jnp.int32
U32 = jnp.uint32

NORM_EPS = 1e-6
ROPE_THETA = 10000.0
PAGE = 128
LANES = 128
SUBLANES = 8
VMEM_LIMIT = 56 * 1024 * 1024
NEG = -1e30
INT_MIN = -2 ** 31

MLA_HEADS, MLA_NOPE, MLA_ROPE, MLA_RANK = 8, 64, 32, 128
DSA_HEADS, DSA_KV_HEADS, DSA_DIM = 8, 2, 64
IDX_HEADS, IDX_DIM = 8, 64
DSA_TOPK = 256
X_HEADS, X_DIM = 4, 64
PEER_HEADS, PEER_KEYS, PEER_TOPK = 8, 128, 16

C_CQ, C_CKV, C_KR, C_DQ, C_DK, C_DV, C_IQ, C_IK, C_IW, C_END = 0, 256, 384, 512, 1024, 1152, 1280, 1792, 1920, 2048


def _cparams(sem):
    return pltpu.CompilerParams(dimension_semantics=sem, vmem_limit_bytes=VMEM_LIMIT)


def _rms(x, g):
    return x * lax.rsqrt(jnp.mean(x * x, axis=-1, keepdims=True) + NORM_EPS) * g


def _dot(a, b):
    return jnp.dot(a.astype(BF16), b.astype(BF16), preferred_element_type=F32)


def _dot_nt(a, b):
    return lax.dot_general(a.astype(BF16), b.astype(BF16), (((1,), (1,)), ((), ())), preferred_element_type=F32)


def _rope(x, cos, sin, d):
    w = x.shape[-1]
    half = d // 2
    lane = lax.broadcasted_iota(I32, x.shape, 1)
    first = (lane & (d - 1)) < half
    fwd = pltpu.roll(x, w - half, axis=1)
    bwd = pltpu.roll(x, half, axis=1)
    return x * cos + jnp.where(first, fwd, bwd) * sin


def _rope_tables(pos, d, rows):
    half = d // 2
    inv = ROPE_THETA ** (-jnp.arange(half, dtype=F32) / half)
    ang = pos.astype(F32)[:, None] * inv[None, :]
    cos, sin = jnp.cos(ang), jnp.sin(ang)
    cos = jnp.tile(jnp.concatenate([cos, cos], -1), (1, LANES // d))
    sin = jnp.tile(jnp.concatenate([-sin, sin], -1), (1, LANES // d))
    reps = max(1, rows // pos.shape[0])
    return jnp.tile(cos, (reps, 1)), jnp.tile(sin, (reps, 1))


def _mix_in_body(x_ref, g_ref, w_ref, qn_ref, wuq_ref, kvn_ref, wuk_ref, c64_ref, s64_ref, c32_ref, s32_ref,
                 ckv_ref, kr_ref, kv_ref, ik_ref, ql_ref, qr_ref, dq_ref, iq_ref, iw_ref):
    xn = _rms(x_ref[...], g_ref[...])
    p = _dot(xn, w_ref[...])
    c64, s64, c32, s32 = c64_ref[...], s64_ref[...], c32_ref[...], s32_ref[...]
    c64x4 = jnp.concatenate([c64] * 4, axis=1)
    s64x4 = jnp.concatenate([s64] * 4, axis=1)
    q = _dot(_rms(p[:, C_CQ:C_CKV], qn_ref[...]), wuq_ref[...])
    nope = MLA_HEADS * MLA_NOPE
    for h in range(MLA_HEADS):
        ql_ref[h] = _dot(q[:, h * MLA_NOPE:(h + 1) * MLA_NOPE], wuk_ref[h])
    qr_ref[...] = _rope(q[:, nope:], jnp.concatenate([c32] * 2, axis=1), jnp.concatenate([s32] * 2, axis=1), MLA_ROPE)
    ckv_ref[...] = _rms(p[:, C_CKV:C_KR], kvn_ref[...])
    kr_ref[...] = _rope(p[:, C_KR:C_DQ], c32, s32, MLA_ROPE)[:, :MLA_ROPE]
    dq_ref[...] = _rope(p[:, C_DQ:C_DK], c64x4, s64x4, DSA_DIM)
    kv_ref[...] = jnp.concatenate([_rope(p[:, C_DK:C_DV], c64, s64, DSA_DIM), p[:, C_DV:C_IQ]], axis=1)
    iq_ref[...] = _rope(p[:, C_IQ:C_IK], c64x4, s64x4, IDX_DIM)
    ik_ref[...] = _rope(p[:, C_IK:C_IW], c64, s64, IDX_DIM)[:, :IDX_DIM]
    iw_ref[...] = p[:, C_IW:C_IW + IDX_HEADS]


def _mixer_in(x, pos, tm, wts):
    n, d = x.shape
    t = pos.shape[0]
    c64, s64 = _rope_tables(pos, DSA_DIM, tm)
    c32, s32 = _rope_tables(pos, MLA_ROPE, tm)
    tb = c64.shape[0] // tm
    row = lambda i: (i, 0)
    tab = lambda i: (i % tb, 0)
    full2 = lambda i: (0, 0)
    full3 = lambda i: (0, 0, 0)
    out_shapes = [
        jax.ShapeDtypeStruct((n, MLA_RANK), F32), jax.ShapeDtypeStruct((n, MLA_ROPE), F32),
        jax.ShapeDtypeStruct((n, 2 * DSA_KV_HEADS * DSA_DIM), F32), jax.ShapeDtypeStruct((n, IDX_DIM), F32),
        jax.ShapeDtypeStruct((MLA_HEADS, n, MLA_RANK), F32), jax.ShapeDtypeStruct((n, MLA_HEADS * MLA_ROPE), F32),
        jax.ShapeDtypeStruct((n, DSA_HEADS * DSA_DIM), F32), jax.ShapeDtypeStruct((n, IDX_HEADS * IDX_DIM), F32),
        jax.ShapeDtypeStruct((n, IDX_HEADS), F32)]
    out_specs = [
        pl.BlockSpec((tm, MLA_RANK), row), pl.BlockSpec((tm, MLA_ROPE), row),
        pl.BlockSpec((tm, 2 * DSA_KV_HEADS * DSA_DIM), row), pl.BlockSpec((tm, IDX_DIM), row),
        pl.BlockSpec((MLA_HEADS, tm, MLA_RANK), lambda i: (0, i, 0)), pl.BlockSpec((tm, MLA_HEADS * MLA_ROPE), row),
        pl.BlockSpec((tm, DSA_HEADS * DSA_DIM), row), pl.BlockSpec((tm, IDX_HEADS * IDX_DIM), row),
        pl.BlockSpec((tm, IDX_HEADS), row)]
    in_specs = [
        pl.BlockSpec((tm, d), row), pl.BlockSpec((1, d), full2), pl.BlockSpec((d, C_END), full2),
        pl.BlockSpec((1, 256), full2), pl.BlockSpec((256, MLA_HEADS * (MLA_NOPE + MLA_ROPE)), full2),
        pl.BlockSpec((1, MLA_RANK), full2), pl.BlockSpec((MLA_HEADS, MLA_NOPE, MLA_RANK), full3),
        pl.BlockSpec((tm, LANES), tab), pl.BlockSpec((tm, LANES), tab),
        pl.BlockSpec((tm, LANES), tab), pl.BlockSpec((tm, LANES), tab)]
    return pl.pallas_call(
        _mix_in_body, grid=(n // tm,), in_specs=in_specs, out_specs=out_specs, out_shape=out_shapes,
        compiler_params=_cparams(("parallel",)), name="mixer_in",
    )(x, wts["norm_mix"], wts["w_in"], wts["q_norm"], wts["w_uq"], wts["kv_norm"], wts["w_ukt"], c64, s64, c32, s32)


def _online_update(m_ref, l_ref, acc_ref, s, mask, v):
    if mask is not None:
        s = jnp.where(mask, s, NEG)
    m_old = m_ref[...]
    m_new = jnp.maximum(m_old, jnp.max(s, axis=1, keepdims=True))
    p = jnp.exp(s - m_new)
    alpha = jnp.exp(m_old - m_new)
    l_ref[...] = alpha * l_ref[...] + jnp.sum(p, axis=1, keepdims=True)
    acc_ref[...] = alpha * acc_ref[...] + _dot(p, v)
    m_ref[...] = m_new


def _flash_step(m_ref, acc_ref, s, v1):
    m_old = m_ref[...]
    m_new = jnp.maximum(m_old, jnp.max(s, axis=1, keepdims=True))
    acc_ref[...] = jnp.exp(m_old - m_new) * acc_ref[...] + _dot(jnp.exp(s - m_new), v1)
    m_ref[...] = m_new


def _with_ones(v):
    return jnp.concatenate([v, jnp.ones(v.shape, v.dtype)], axis=1)


def _mask_heads(s, mask, group):
    r, k = s.shape
    return jnp.where(mask[None], s.reshape(group, r // group, k), NEG).reshape(r, k)


def _init_state(m_ref, l_ref, acc_ref):
    m_ref[...] = jnp.full(m_ref.shape, NEG, F32)
    l_ref[...] = jnp.zeros(l_ref.shape, F32)
    acc_ref[...] = jnp.zeros(acc_ref.shape, F32)


def _stack_heads(x, n_heads, width):
    return jnp.concatenate([x[:, h * width:(h + 1) * width] for h in range(n_heads)], axis=0)


def _mla_prompt_body(ql_ref, qr_ref, ckv_ref, kr_ref, o_ref, m_ref, l_ref, acc_ref, *, tq, tk, sub):
    i = pl.program_id(1)
    rows = MLA_HEADS * tq
    scale = (MLA_NOPE + MLA_ROPE) ** -0.5
    q_l = (ql_ref[...].reshape(rows, MLA_RANK) * scale).astype(BF16)
    q_r = (_stack_heads(qr_ref[...], MLA_HEADS, MLA_ROPE) * scale).astype(BF16)
    _init_state(m_ref, l_ref, acc_ref)

    def block(j, masked):
        k0 = pl.multiple_of(j * tk, tk)
        kc = ckv_ref[pl.ds(k0, tk), :].astype(BF16)
        kr = kr_ref[pl.ds(k0, tk), :].astype(BF16)
        for r0 in range(0, rows, sub):
            s = _dot_nt(q_l[r0:r0 + sub], kc) + _dot_nt(q_r[r0:r0 + sub], kr)
            mask = None
            if masked:
                row_t = i * tq + ((r0 + lax.broadcasted_iota(I32, (sub, 1), 0)) & (tq - 1))
                mask = (k0 + lax.broadcasted_iota(I32, (sub, tk), 1)) <= row_t
            sl = pl.ds(r0, sub)
            _online_update(m_ref.at[sl], l_ref.at[sl], acc_ref.at[sl], s, mask, kc)

    n_full = (i * tq + 1) // tk

    def step(j, carry):
        block(j, False)
        return carry

    lax.fori_loop(0, n_full, step, 0)
    block(n_full, True)
    o_ref[...] = (acc_ref[...] / l_ref[...]).reshape(MLA_HEADS, tq, MLA_RANK)


def _mla_prompt(ql, qr, ckv, kr, b, t, tq, tk):
    n = b * t
    nq = t // tq
    rows = MLA_HEADS * tq
    assert tk % tq == 0
    return pl.pallas_call(
        functools.partial(_mla_prompt_body, tq=tq, tk=tk, sub=min(rows, 256)),
        grid=(b, nq),
        in_specs=[pl.BlockSpec((MLA_HEADS, tq, MLA_RANK), lambda bi, i: (0, bi * nq + i, 0)),
                  pl.BlockSpec((tq, MLA_HEADS * MLA_ROPE), lambda bi, i: (bi * nq + i, 0)),
                  pl.BlockSpec((t, MLA_RANK), lambda bi, i: (bi, 0)),
                  pl.BlockSpec((t, MLA_ROPE), lambda bi, i: (bi, 0))],
        out_specs=pl.BlockSpec((MLA_HEADS, tq, MLA_RANK), lambda bi, i: (0, bi * nq + i, 0)),
        out_shape=jax.ShapeDtypeStruct((MLA_HEADS, n, MLA_RANK), F32),
        scratch_shapes=[pltpu.VMEM((rows, 1), F32), pltpu.VMEM((rows, 1), F32), pltpu.VMEM((rows, MLA_RANK), F32)],
        compiler_params=_cparams(("parallel", "arbitrary")), name="mla_prompt",
    )(ql, qr, ckv, kr)


def _order_key(score):
    bits = lax.bitcast_convert_type(score + 0.0, I32)
    return bits ^ ((bits >> 31) & 0x7FFFFFFF)


def _kth_largest(count_ge, shape, k):
    kf = float(k)
    tau0 = jnp.where(count_ge(jnp.zeros(shape, I32)) >= kf, 0, INT_MIN).astype(I32)

    def bit(it, tau):
        cand = tau | lax.shift_left(jnp.int32(1), 30 - it)
        return jnp.where(count_ge(cand) >= kf, cand, tau)

    return lax.fori_loop(0, 31, bit, tau0)


def _select_mask(key, valid, tau, need, carry, tri):
    masks = []
    for c in range(key.shape[1] // LANES):
        kb = key[:, c * LANES:(c + 1) * LANES]
        eq, gt = kb == tau, kb > tau
        if valid is not None:
            vb = valid[:, c * LANES:(c + 1) * LANES]
            eq, gt = eq & vb, gt & vb
        eqf = jnp.where(eq, 1.0, 0.0)
        before = carry + jnp.dot(eqf.astype(BF16), tri, preferred_element_type=F32)
        masks.append(gt | (eq & (before < need)))
        carry = carry + jnp.sum(eqf, axis=1, keepdims=True)
    return jnp.concatenate(masks, axis=1), carry


def _strict_upper(n):
    r = lax.broadcasted_iota(I32, (n, n), 0)
    c = lax.broadcasted_iota(I32, (n, n), 1)
    return jnp.where(r < c, 1.0, 0.0).astype(BF16)


def _index_scores(iq_s, iw_b, ikc):
    t = iq_s.shape[0] // IDX_HEADS
    d = _dot_nt(iq_s, ikc)
    score = None
    for h in range(IDX_HEADS):
        term = iw_b(h) * jnp.maximum(d[h * t:(h + 1) * t], 0.0)
        score = term if score is None else score + term
    return score


def _count_lanes(m):
    part = m[:, :LANES]
    for c in range(1, m.shape[1] // LANES):
        part = part + m[:, c * LANES:(c + 1) * LANES]
    return part


def _dsa_prompt_body(iq_ref, iw_ref, ik_ref, dq_ref, kv_ref, o_ref, key_ref, iwb_ref, m_ref, acc_ref, *, tq, ck, topk):
    i = pl.program_id(1)
    nc = ((i + 1) * tq + ck - 1) // ck
    group = DSA_HEADS // DSA_KV_HEADS
    iq_s = _stack_heads(iq_ref[...], IDX_HEADS, IDX_DIM).astype(BF16)
    iw = iw_ref[...]
    for h in range(IDX_HEADS):
        iwb_ref[h] = jnp.broadcast_to(iw[:, h:h + 1], (tq, LANES))
    iw_b = lambda h: jnp.concatenate([iwb_ref[h]] * (ck // LANES), axis=1)
    q_t = i * tq + lax.broadcasted_iota(I32, (tq, 1), 0)

    def causal(c):
        return (c * ck + lax.broadcasted_iota(I32, (tq, ck), 1)) <= q_t

    def score_chunk(c, carry):
        k0 = pl.multiple_of(c * ck, ck)
        score = _index_scores(iq_s, iw_b, ik_ref[pl.ds(k0, ck), :].astype(BF16))
        key_ref[c] = jnp.where(causal(c), _order_key(score), INT_MIN)
        return carry

    lax.fori_loop(0, nc, score_chunk, 0)

    def count(pred):
        def body(c, acc):
            return acc + _count_lanes(jnp.where(pred(key_ref[c]), 1.0, 0.0))
        return jnp.sum(lax.fori_loop(0, nc, body, jnp.zeros((tq, LANES), F32)), axis=1, keepdims=True)

    tau = _kth_largest(lambda cand: count(lambda k: k >= cand), (tq, 1), topk)
    need = float(topk) - count(lambda k: k > tau)
    tri = _strict_upper(LANES)
    dq = dq_ref[...] * DSA_DIM ** -0.5
    qs = [_stack_heads(dq[:, g * group * DSA_DIM:(g + 1) * group * DSA_DIM], group, DSA_DIM).astype(BF16)
          for g in range(DSA_KV_HEADS)]
    m_ref[...] = jnp.full(m_ref.shape, NEG, F32)
    acc_ref[...] = jnp.zeros(acc_ref.shape, F32)

    def attend(c, carry):
        k0 = pl.multiple_of(c * ck, ck)
        mask, carry = _select_mask(key_ref[c], causal(c), tau, need, carry, tri)
        kvc = kv_ref[pl.ds(k0, ck), :].astype(BF16)
        for g in range(DSA_KV_HEADS):
            kc = kvc[:, g * DSA_DIM:(g + 1) * DSA_DIM]
            vc = kvc[:, (DSA_KV_HEADS + g) * DSA_DIM:(DSA_KV_HEADS + g + 1) * DSA_DIM]
            s = _mask_heads(_dot_nt(qs[g], kc), mask, group)
            _flash_step(m_ref.at[g], acc_ref.at[g], s, _with_ones(vc))
        return carry

    lax.fori_loop(0, nc, attend, jnp.zeros((tq, 1), F32))
    outs = []
    for g in range(DSA_KV_HEADS):
        acc = acc_ref[g]
        o = acc[:, :DSA_DIM] / acc[:, DSA_DIM:DSA_DIM + 1]
        outs += [o[n * tq:(n + 1) * tq] for n in range(group)]
    o_ref[...] = jnp.concatenate(outs, axis=1)


def _dsa_prompt(iq, iw, ik, dq, kv, b, t, tq, ck, topk):
    n = b * t
    nq = t // tq
    group = DSA_HEADS // DSA_KV_HEADS
    rows = group * tq
    qrow = lambda bi, i: (bi * nq + i, 0)
    brow = lambda bi, i: (bi, 0)
    return pl.pallas_call(
        functools.partial(_dsa_prompt_body, tq=tq, ck=ck, topk=topk),
        grid=(b, nq),
        in_specs=[pl.BlockSpec((tq, IDX_HEADS * IDX_DIM), qrow), pl.BlockSpec((tq, IDX_HEADS), qrow),
                  pl.BlockSpec((t, IDX_DIM), brow), pl.BlockSpec((tq, DSA_HEADS * DSA_DIM), qrow),
                  pl.BlockSpec((t, 2 * DSA_KV_HEADS * DSA_DIM), brow)],
        out_specs=pl.BlockSpec((tq, DSA_HEADS * DSA_DIM), qrow),
        out_shape=jax.ShapeDtypeStruct((n, DSA_HEADS * DSA_DIM), F32),
        scratch_shapes=[pltpu.VMEM((t // ck, tq, ck), I32), pltpu.VMEM((IDX_HEADS, tq, LANES), F32),
                        pltpu.VMEM((DSA_KV_HEADS, rows, 1), F32), pltpu.VMEM((DSA_KV_HEADS, rows, 2 * DSA_DIM), F32)],
        compiler_params=_cparams(("parallel", "arbitrary")), name="dsa_prompt",
    )(iq, iw, ik, dq, kv)


def _mla_sample_body(pt_ref, ql_ref, qr_ref, ckvn_ref, krn_ref, *rest, t, g_pages):
    ckv_pages = rest[:g_pages]
    kr_pages = rest[g_pages:2 * g_pages]
    o_ref, m_ref, l_ref, acc_ref = rest[2 * g_pages:]
    j = pl.program_id(1)
    rows = MLA_HEADS * t
    scale = (MLA_NOPE + MLA_ROPE) ** -0.5
    q_l = (ql_ref[...].reshape(rows, MLA_RANK) * scale).astype(BF16)
    q_r = (_stack_heads(qr_ref[...], MLA_HEADS, MLA_ROPE) * scale).astype(BF16)

    @pl.when(j == 0)
    def _():
        _init_state(m_ref, l_ref, acc_ref)

    kc = jnp.concatenate([r[0] for r in ckv_pages], axis=0).astype(BF16)
    kr = jnp.concatenate([r[0] for r in kr_pages], axis=0).astype(BF16)
    _online_update(m_ref, l_ref, acc_ref, _dot_nt(q_l, kc) + _dot_nt(q_r, kr), None, kc)

    @pl.when(j == pl.num_programs(1) - 1)
    def _():
        kn = ckvn_ref[0].astype(BF16)
        s = _dot_nt(q_l, kn) + _dot_nt(q_r, krn_ref[0].astype(BF16))
        row_t = lax.broadcasted_iota(I32, (rows, PAGE), 0) & (t - 1)
        _online_update(m_ref, l_ref, acc_ref, s, lax.broadcasted_iota(I32, (rows, PAGE), 1) <= row_t, kn)
        o_ref[...] = (acc_ref[...] / l_ref[...]).reshape(MLA_HEADS, t, MLA_RANK)


def _pad_new(x, b, t):
    return jnp.pad(x.reshape(b, t, x.shape[-1]), ((0, 0), (0, PAGE - t), (0, 0)))


def _mla_sample(page_table, ql, qr, ckv_new, kr_new, pool_ckv, pool_kr, b, t, g_pages):
    n_pages = page_table.shape[1]
    steps = n_pages // g_pages
    rows = MLA_HEADS * t
    page = lambda g: (lambda bi, j, pt: (pt[bi, j * g_pages + g], 0, 0))
    seq3 = lambda bi, j, pt: (bi, 0, 0)
    in_specs = [pl.BlockSpec((MLA_HEADS, t, MLA_RANK), lambda bi, j, pt: (0, bi, 0)),
                pl.BlockSpec((t, MLA_HEADS * MLA_ROPE), lambda bi, j, pt: (bi, 0)),
                pl.BlockSpec((1, PAGE, MLA_RANK), seq3), pl.BlockSpec((1, PAGE, MLA_ROPE), seq3)]
    in_specs += [pl.BlockSpec((1, PAGE, MLA_RANK), page(g)) for g in range(g_pages)]
    in_specs += [pl.BlockSpec((1, PAGE, MLA_ROPE), page(g)) for g in range(g_pages)]
    grid_spec = pltpu.PrefetchScalarGridSpec(
        num_scalar_prefetch=1, grid=(b, steps), in_specs=in_specs,
        out_specs=pl.BlockSpec((MLA_HEADS, t, MLA_RANK), lambda bi, j, pt: (0, bi, 0)),
        scratch_shapes=[pltpu.VMEM((rows, 1), F32), pltpu.VMEM((rows, 1), F32), pltpu.VMEM((rows, MLA_RANK), F32)])
    return pl.pallas_call(
        functools.partial(_mla_sample_body, t=t, g_pages=g_pages), grid_spec=grid_spec,
        out_shape=jax.ShapeDtypeStruct((MLA_HEADS, b * t, MLA_RANK), F32),
        compiler_params=_cparams(("parallel", "arbitrary")), name="mla_sample",
    )(page_table, ql, qr, _pad_new(ckv_new, b, t), _pad_new(kr_new, b, t),
      *([pool_ckv] * g_pages), *([pool_kr] * g_pages))


def _dsa_sample_body(pt_ref, iq_ref, iw_ref, dq_ref, ikn_ref, kvn_ref, *rest, t, g_pages, steps, topk):
    ik_pages = rest[:g_pages]
    kv_pages = rest[g_pages:2 * g_pages]
    o_ref, key_ref, keyn_ref, tau_ref, need_ref, carry_ref, m_ref, acc_ref = rest[2 * g_pages:]
    j = pl.program_id(1)
    ck = g_pages * PAGE
    group = DSA_HEADS // DSA_KV_HEADS
    scale = DSA_DIM ** -0.5
    new_valid = lax.broadcasted_iota(I32, (t, PAGE), 1) <= lax.broadcasted_iota(I32, (t, PAGE), 0)

    @pl.when(j < steps)
    def _():
        iq_s = _stack_heads(iq_ref[...], IDX_HEADS, IDX_DIM).astype(BF16)
        iw = iw_ref[...]
        ikc = jnp.concatenate([r[0] for r in ik_pages], axis=0).astype(BF16)
        key_ref[j] = _order_key(_index_scores(iq_s, lambda h: jnp.broadcast_to(iw[:, h:h + 1], (t, ck)), ikc))

        @pl.when(j == steps - 1)
        def _():
            sn = _index_scores(iq_s, lambda h: jnp.broadcast_to(iw[:, h:h + 1], (t, PAGE)), ikn_ref[0].astype(BF16))
            keyn_ref[...] = jnp.where(new_valid, _order_key(sn), INT_MIN)

    @pl.when(j == steps)
    def _():
        def count(pred):
            acc = jnp.where(pred(keyn_ref[...]), 1.0, 0.0)
            for c in range(steps):
                acc = acc + _count_lanes(jnp.where(pred(key_ref[c]), 1.0, 0.0))
            return jnp.sum(acc, axis=1, keepdims=True)

        tau = _kth_largest(lambda cand: count(lambda k: k >= cand), (t, 1), topk)
        tau_ref[...] = tau
        need_ref[...] = float(topk) - count(lambda k: k > tau)
        carry_ref[...] = jnp.zeros((t, 1), F32)
        m_ref[...] = jnp.full(m_ref.shape, NEG, F32)
        acc_ref[...] = jnp.zeros(acc_ref.shape, F32)

    def attend(key, valid, kvc, tri):
        mask, carry = _select_mask(key, valid, tau_ref[...], need_ref[...], carry_ref[...], tri)
        carry_ref[...] = carry
        dq = dq_ref[...] * scale
        for g in range(DSA_KV_HEADS):
            qs = _stack_heads(dq[:, g * group * DSA_DIM:(g + 1) * group * DSA_DIM], group, DSA_DIM)
            kc = kvc[:, g * DSA_DIM:(g + 1) * DSA_DIM]
            vc = kvc[:, (DSA_KV_HEADS + g) * DSA_DIM:(DSA_KV_HEADS + g + 1) * DSA_DIM]
            s = _mask_heads(_dot_nt(qs, kc), mask, group)
            _flash_step(m_ref.at[g], acc_ref.at[g], s, _with_ones(vc))

    @pl.when(j >= steps)
    def _():
        kvc = jnp.concatenate([r[0] for r in kv_pages], axis=0).astype(BF16)
        attend(key_ref[j - steps], None, kvc, _strict_upper(LANES))

        @pl.when(j == 2 * steps - 1)
        def _():
            attend(keyn_ref[...], new_valid, kvn_ref[0].astype(BF16), _strict_upper(LANES))
            outs = []
            for g in range(DSA_KV_HEADS):
                acc = acc_ref[g]
                o = acc[:, :DSA_DIM] / acc[:, DSA_DIM:DSA_DIM + 1]
                outs += [o[n * t:(n + 1) * t] for n in range(group)]
            o_ref[...] = jnp.concatenate(outs, axis=1)


def _dsa_sample(page_table, iq, iw, dq, ik_new, kv_new, pool_ik, pool_kv, b, t, g_pages, topk):
    n_pages = page_table.shape[1]
    steps = n_pages // g_pages
    group = DSA_HEADS // DSA_KV_HEADS
    rows = group * t
    kvw = 2 * DSA_KV_HEADS * DSA_DIM
    ik_page = lambda g: (lambda bi, j, pt: (pt[bi, jnp.minimum(j, steps - 1) * g_pages + g], 0, 0))
    kv_page = lambda g: (lambda bi, j, pt: (pt[bi, jnp.maximum(j - steps, 0) * g_pages + g], 0, 0))
    seq2 = lambda bi, j, pt: (bi, 0)
    seq3 = lambda bi, j, pt: (bi, 0, 0)
    in_specs = [pl.BlockSpec((t, IDX_HEADS * IDX_DIM), seq2), pl.BlockSpec((t, IDX_HEADS), seq2),
                pl.BlockSpec((t, DSA_HEADS * DSA_DIM), seq2),
                pl.BlockSpec((1, PAGE, IDX_DIM), seq3), pl.BlockSpec((1, PAGE, kvw), seq3)]
    in_specs += [pl.BlockSpec((1, PAGE, IDX_DIM), ik_page(g)) for g in range(g_pages)]
    in_specs += [pl.BlockSpec((1, PAGE, kvw), kv_page(g)) for g in range(g_pages)]
    grid_spec = pltpu.PrefetchScalarGridSpec(
        num_scalar_prefetch=1, grid=(b, 2 * steps), in_specs=in_specs,
        out_specs=pl.BlockSpec((t, DSA_HEADS * DSA_DIM), seq2),
        scratch_shapes=[pltpu.VMEM((steps, t, g_pages * PAGE), I32), pltpu.VMEM((t, PAGE), I32),
                        pltpu.VMEM((t, 1), I32), pltpu.VMEM((t, 1), F32), pltpu.VMEM((t, 1), F32),
                        pltpu.VMEM((DSA_KV_HEADS, rows, 1), F32), pltpu.VMEM((DSA_KV_HEADS, rows, 2 * DSA_DIM), F32)])
    return pl.pallas_call(
        functools.partial(_dsa_sample_body, t=t, g_pages=g_pages, steps=steps, topk=topk), grid_spec=grid_spec,
        out_shape=jax.ShapeDtypeStruct((b * t, DSA_HEADS * DSA_DIM), F32),
        compiler_params=_cparams(("parallel", "arbitrary")), name="dsa_sample",
    )(page_table, iq, iw, dq, _pad_new(ik_new, b, t), _pad_new(kv_new, b, t),
      *([pool_ik] * g_pages), *([pool_kv] * g_pages))


def _mix_out_body(ol_ref, od_ref, h_ref, wuv_ref, wo_ref, o_ref):
    mla = jnp.concatenate([_dot(ol_ref[h], wuv_ref[h]) for h in range(MLA_HEADS)], axis=1)
    half = mla.shape[1]
    o_ref[...] = h_ref[...] + _dot(mla, wo_ref[:half, :]) + _dot(od_ref[...], wo_ref[half:, :])


def _mixer_out(o_lat, o_dsa, h, wts, tm):
    n, d = h.shape
    row = lambda i: (i, 0)
    v_dim = wts["w_uvh"].shape[-1]
    mix = wts["w_o"].shape[0]
    return pl.pallas_call(
        _mix_out_body, grid=(n // tm,),
        in_specs=[pl.BlockSpec((MLA_HEADS, tm, MLA_RANK), lambda i: (0, i, 0)), pl.BlockSpec((tm, DSA_HEADS * DSA_DIM), row),
                  pl.BlockSpec((tm, d), row), pl.BlockSpec((MLA_HEADS, MLA_RANK, v_dim), lambda i: (0, 0, 0)),
                  pl.BlockSpec((mix, d), lambda i: (0, 0))],
        out_specs=pl.BlockSpec((tm, d), row), out_shape=jax.ShapeDtypeStruct((n, d), F32),
        compiler_params=_cparams(("parallel",)), name="mixer_out",
    )(o_lat, o_dsa, h, wts["w_uvh"], wts["w_o"])


def _mem_kv_body(m_ref, g_ref, w_ref, k_ref, v_ref):
    p = _dot(_rms(m_ref[...], g_ref[...]), w_ref[...])
    half = p.shape[1] // 2
    k_ref[...] = p[:, :half]
    v_ref[...] = p[:, half:]


def _mem_kv(mem, wts, tm):
    n, d = mem.shape
    w = wts["w_kvx"]
    half = w.shape[1] // 2
    row = lambda i: (i, 0)
    return pl.pallas_call(
        _mem_kv_body, grid=(n // tm,),
        in_specs=[pl.BlockSpec((tm, d), row), pl.BlockSpec((1, d), lambda i: (0, 0)), pl.BlockSpec(w.shape, lambda i: (0, 0))],
        out_specs=[pl.BlockSpec((tm, half), row)] * 2, out_shape=[jax.ShapeDtypeStruct((n, half), F32)] * 2,
        compiler_params=_cparams(("parallel",)), name="mem_kv",
    )(mem, wts["mem_norm"], w)


def _cross_body(h_ref, g_ref, wq_ref, k_ref, v_ref, wo_ref, o_ref):
    h = h_ref[...]
    q = _dot(_rms(h, g_ref[...]), wq_ref[...])
    k = k_ref[0].astype(BF16)
    v = v_ref[0].astype(BF16)
    outs = []
    for hd in range(X_HEADS):
        sl = slice(hd * X_DIM, (hd + 1) * X_DIM)
        s = _dot_nt(q[:, sl], k[:, sl]) * X_DIM ** -0.5
        e = jnp.exp(s - jnp.max(s, axis=1, keepdims=True))
        p = e / jnp.sum(e, axis=1, keepdims=True)
        outs.append(_dot(p, v[:, sl]))
    o_ref[...] = h + _dot(jnp.concatenate(outs, axis=1), wo_ref[...])


def _cross(h, mk, mv, wts, b, tm):
    n, d = h.shape
    per = (n // b) // tm
    m, w = mk.shape[1], mk.shape[2]
    row = lambda i: (i, 0)
    mem = lambda i: (i // per, 0, 0)
    return pl.pallas_call(
        _cross_body, grid=(n // tm,),
        in_specs=[pl.BlockSpec((tm, d), row), pl.BlockSpec((1, d), lambda i: (0, 0)), pl.BlockSpec((d, w), lambda i: (0, 0)),
                  pl.BlockSpec((1, m, w), mem), pl.BlockSpec((1, m, w), mem), pl.BlockSpec((w, d), lambda i: (0, 0))],
        out_specs=pl.BlockSpec((tm, d), row), out_shape=jax.ShapeDtypeStruct((n, d), F32),
        compiler_params=_cparams(("parallel",)), name="cross_attn",
    )(h, wts["norm_mem"], wts["wq_x"], mk, mv, wts["wo_x"])


def _extract_topk(s, payload, k):
    r = s.shape[0]
    row = lax.broadcasted_iota(I32, s.shape, 0)
    vals, pays = [], []
    for _ in range(k):
        m = jnp.max(s, axis=0, keepdims=True)
        first = jnp.min(jnp.where(s == m, row, r), axis=0, keepdims=True)
        hit = row == first
        vals.append(m)
        pays.append(first if payload is None else jnp.sum(jnp.where(hit, payload, 0), axis=0, keepdims=True))
        s = jnp.where(hit, -jnp.inf, s)
    return jnp.concatenate(vals, axis=0), jnp.concatenate(pays, axis=0)


def _pair_blocks(k):
    blocks = []
    a = 0
    while a < k and k // (a + 1) > 1:
        nb = k // (a + 1)
        blocks.append((a, nb, -(-nb // SUBLANES) * SUBLANES))
        a += 1
    return blocks, a


ROUTE_HEADS_PER_ITER = 2


def _peer_route_body(h_ref, g_ref, wpq_ref, keys_ref, xn_ref, e_ref, gate_ref, q_scr, gs_scr, ge_scr, *, row_words):
    tm = h_ref.shape[0]
    k = PEER_TOPK
    xn = _rms(h_ref[...], g_ref[...])
    xn_ref[...] = xn
    q = _dot(xn, wpq_ref[...])
    n_sub = 2 * PEER_HEADS
    kd = q.shape[1] // n_sub
    for hp in range(n_sub):
        q_scr[hp] = q[:, hp * kd:(hp + 1) * kd].astype(BF16)
    blocks, a_tail = _pair_blocks(k)

    def route_head(h):
        (t1, i1), (t2, i2) = [_extract_topk(_dot_nt(keys_ref[2 * h + p], q_scr[2 * h + p]), None, k) for p in range(2)]
        cand, ids = [], []
        for a, nb, rows in blocks:
            live = lax.broadcasted_iota(I32, (rows, tm), 0) < nb
            cand.append(jnp.where(live, t1[a:a + 1] + t2[:rows], -jnp.inf))
            ids.append(i1[a:a + 1] * PEER_KEYS + i2[:rows])
        cand.append(t1[a_tail:] + t2[0:1])
        ids.append(i1[a_tail:] * PEER_KEYS + i2[0:1])
        gs_scr[h], ge_scr[h] = _extract_topk(jnp.concatenate(cand, axis=0), jnp.concatenate(ids, axis=0), k)

    def heads(it, carry):
        for j in range(ROUTE_HEADS_PER_ITER):
            route_head(it * ROUTE_HEADS_PER_ITER + j)
        return carry

    lax.fori_loop(0, PEER_HEADS // ROUTE_HEADS_PER_ITER, heads, 0)
    gs = gs_scr[...]
    e = jnp.exp(gs - jnp.max(gs, axis=1, keepdims=True))
    gate = e / jnp.sum(e, axis=1, keepdims=True)
    gate_ref[...] = gate.reshape(PEER_HEADS * k, tm).T
    e_ref[...] = (ge_scr[...] * row_words).reshape(PEER_HEADS * k, tm).T


def _peer_route(h, wts, tm, row_words):
    n, d = h.shape
    slots = PEER_HEADS * PEER_TOPK
    keys = wts["peer_keys"]
    n_sub, n_keys, kd = keys.shape
    row = lambda i: (i, 0)
    return pl.pallas_call(
        functools.partial(_peer_route_body, row_words=row_words), grid=(n // tm,),
        in_specs=[pl.BlockSpec((tm, d), row), pl.BlockSpec((1, d), lambda i: (0, 0)),
                  pl.BlockSpec((d, n_sub * kd), lambda i: (0, 0)), pl.BlockSpec(keys.shape, lambda i: (0, 0, 0))],
        out_specs=[pl.BlockSpec((tm, d), row), pl.BlockSpec((tm, slots), row), pl.BlockSpec((tm, slots), row)],
        out_shape=[jax.ShapeDtypeStruct((n, d), F32), jax.ShapeDtypeStruct((n, slots), I32),
                   jax.ShapeDtypeStruct((n, slots), F32)],
        scratch_shapes=[pltpu.VMEM((n_sub, tm, kd), BF16), pltpu.VMEM((PEER_HEADS, PEER_TOPK, tm), F32),
                        pltpu.VMEM((PEER_HEADS, PEER_TOPK, tm), I32)],
        compiler_params=_cparams(("parallel",)), name="peer_route",
    )(h, wts["norm_ffn"], wts["w_pq"], keys)


def _pack_table(w):
    e, d = w.shape
    bits = lax.bitcast_convert_type(w.astype(jnp.bfloat16), jnp.uint16).astype(U32)
    packed = bits[:, :d // 2] | (bits[:, d // 2:] << 16)
    return packed.reshape(e * d // (2 * LANES), LANES)


def _unpack_row(tbl_ref, e_row, rows):
    w = tbl_ref[pl.ds(pl.multiple_of(e_row, rows), rows), :]
    lo = lax.bitcast_convert_type(w << 16, F32)
    hi = lax.bitcast_convert_type(w & jnp.uint32(0xFFFF0000), F32)
    return lo, hi


def _load_table(tbl_hbm, tbl_ref, sem):
    @pl.when(pl.program_id(0) == 0)
    def _():
        cp = pltpu.make_async_copy(tbl_hbm, tbl_ref, sem)
        cp.start()
        cp.wait()


TOKENS_PER_ITER = 2


def _peer_up_body(e_ref, x_ref, gate_ref, tbl_hbm, a_ref, tbl_ref, p_ref, t_ref, sem, *, tb, slots, rows):
    _load_table(tbl_hbm, tbl_ref, sem)

    def tokens(it, carry):
        for j in range(TOKENS_PER_ITER):
            n = it * TOKENS_PER_ITER + j
            xv = x_ref[pl.ds(pl.multiple_of(n * 2 * rows, 2 * rows), 2 * rows), :]
            xa, xb = xv[:rows], xv[rows:]
            for k in range(slots):
                lo, hi = _unpack_row(tbl_ref, e_ref[n, k], rows)
                p_ref[j, pl.ds(k * rows, rows), :] = lo * xa + hi * xb
            t = p_ref[j, pl.ds(0, slots, stride=rows), :]
            for r in range(1, rows):
                t = t + p_ref[j, pl.ds(r, slots, stride=rows), :]
            t_ref[n] = t
        return carry

    lax.fori_loop(0, tb // TOKENS_PER_ITER, tokens, 0)
    for n in range(tb):
        hrow = jnp.sum(t_ref[n].T, axis=0, keepdims=True)
        act = 0.5 * hrow * (1.0 + lax.erf(hrow * (2.0 ** -0.5)))
        a_ref[n] = gate_ref[n] * act


def _peer_down_body(e_ref, a_ref, h_ref, g_ref, tbl_hbm, y_ref, tbl_ref, s_ref, sem, *, tb, slots, rows):
    _load_table(tbl_hbm, tbl_ref, sem)
    d = 2 * rows * LANES
    for n in range(tb):
        s_ref[n] = jnp.broadcast_to(a_ref[n], (slots, slots)).T

    def tokens(it, carry):
        for j in range(TOKENS_PER_ITER):
            n = it * TOKENS_PER_ITER + j
            acc = [jnp.zeros((rows, LANES), F32) for _ in range(4)]
            for k in range(slots):
                lo, hi = _unpack_row(tbl_ref, e_ref[n, k], rows)
                a = jnp.broadcast_to(s_ref[n, pl.ds(k, 1), :], (rows, LANES))
                acc[2 * (k % 2)] = acc[2 * (k % 2)] + a * lo
                acc[2 * (k % 2) + 1] = acc[2 * (k % 2) + 1] + a * hi
            r0 = pl.multiple_of(n * 2 * rows, 2 * rows)
            y_ref[pl.ds(r0, 2 * rows), :] = (h_ref[pl.ds(r0, 2 * rows), :]
                                             + jnp.concatenate([acc[0] + acc[2], acc[1] + acc[3]], axis=0))
        return carry

    lax.fori_loop(0, tb // TOKENS_PER_ITER, tokens, 0)
    h = y_ref[...].reshape(tb, 2 * rows, LANES)
    ss = jnp.sum(jnp.sum(h * h, axis=2, keepdims=True), axis=1, keepdims=True)
    y_ref[...] = (h * lax.rsqrt(ss / d + NORM_EPS) * g_ref[...][None]).reshape(tb * 2 * rows, LANES)


def _peer_experts(xn, h, e, gate, wts, tb):
    n, d = h.shape
    slots = e.shape[1]
    assert slots == LANES
    rows = d // (2 * LANES)
    vr = 2 * rows
    tok = lambda i: (i, 0)
    smem = functools.partial(pl.BlockSpec, memory_space=pltpu.SMEM)
    tbl_shape = wts["peer_u"].shape
    common = dict(grid=(n // tb,), compiler_params=_cparams(("arbitrary",)))
    tok3 = lambda i: (i, 0, 0)
    a = pl.pallas_call(
        functools.partial(_peer_up_body, tb=tb, slots=slots, rows=rows),
        in_specs=[smem((tb, slots), tok), pl.BlockSpec((tb * vr, LANES), tok), pl.BlockSpec((tb, 1, slots), tok3),
                  pl.BlockSpec(memory_space=pl.ANY)],
        out_specs=pl.BlockSpec((tb, 1, slots), tok3), out_shape=jax.ShapeDtypeStruct((n, 1, slots), F32),
        scratch_shapes=[pltpu.VMEM(tbl_shape, U32), pltpu.VMEM((TOKENS_PER_ITER, slots * rows, LANES), F32),
                        pltpu.VMEM((tb, slots, LANES), F32), pltpu.SemaphoreType.DMA(())],
        name="peer_up", **common,
    )(e, xn.reshape(n * vr, LANES), gate.reshape(n, 1, slots), wts["peer_u"])
    y = pl.pallas_call(
        functools.partial(_peer_down_body, tb=tb, slots=slots, rows=rows),
        in_specs=[smem((tb, slots), tok), pl.BlockSpec((tb, 1, slots), tok3), pl.BlockSpec((tb * vr, LANES), tok),
                  pl.BlockSpec((vr, LANES), lambda i: (0, 0)), pl.BlockSpec(memory_space=pl.ANY)],
        out_specs=pl.BlockSpec((tb * vr, LANES), tok), out_shape=jax.ShapeDtypeStruct((n * vr, LANES), F32),
        scratch_shapes=[pltpu.VMEM(tbl_shape, U32), pltpu.VMEM((tb, slots, LANES), F32), pltpu.SemaphoreType.DMA(())],
        name="peer_down", **common,
    )(e, a, h.reshape(n * vr, LANES), wts["norm_final"].reshape(vr, LANES), wts["peer_v"])
    return y.reshape(n, d)


def _prep_weights(norm_mix, w_in, mla_q_norm, w_uq, mla_kv_norm, w_uk, w_uv, w_o, norm_mem, mem_norm, wq_x, wk_x,
                  wv_x, wo_x, norm_ffn, w_pq, peer_keys, peer_u, peer_v, norm_final):
    d = w_in.shape[0]
    widths = (256, 128, 32, 512, 128, 128, 512, 8, 64)
    offs = np.concatenate([[0], np.cumsum(widths)])
    cq, ckv, kr, dq, dk, dv, iq, iw, ik = [w_in[:, offs[i]:offs[i + 1]] for i in range(9)]
    pad = lambda w, to: jnp.pad(w, ((0, 0), (0, to - w.shape[1])))
    fused = jnp.concatenate([cq, ckv, pad(kr, 128), dq, dk, dv, iq, pad(ik, 128), pad(iw, 128)], axis=1)
    hd = MLA_NOPE + MLA_ROPE
    uq = w_uq.reshape(w_uq.shape[0], MLA_HEADS, hd)
    uq = jnp.concatenate([uq[:, :, :MLA_NOPE].reshape(-1, MLA_HEADS * MLA_NOPE),
                          uq[:, :, MLA_NOPE:].reshape(-1, MLA_HEADS * MLA_ROPE)], axis=1)
    row = lambda v: v.reshape(1, -1)
    n_sub = peer_keys.shape[0] * peer_keys.shape[1]
    return dict(
        norm_mix=row(norm_mix), w_in=fused.astype(BF16), q_norm=row(mla_q_norm), w_uq=uq.astype(BF16),
        kv_norm=row(mla_kv_norm), w_ukt=jnp.transpose(w_uk, (1, 2, 0)).astype(BF16),
        w_uvh=jnp.transpose(w_uv, (1, 0, 2)).astype(BF16), w_o=w_o.astype(BF16),
        norm_mem=row(norm_mem), mem_norm=row(mem_norm), wq_x=wq_x.astype(BF16),
        w_kvx=jnp.concatenate([wk_x, wv_x], axis=1).astype(BF16), wo_x=wo_x.astype(BF16),
        norm_ffn=row(norm_ffn), w_pq=w_pq.astype(BF16),
        peer_keys=peer_keys.reshape(n_sub, peer_keys.shape[2], peer_keys.shape[3]).astype(BF16),
        peer_u=_pack_table(peer_u), peer_v=_pack_table(peer_v), norm_final=norm_final)


def _tile(n, want):
    t = min(n, want)
    while n % t:
        t //= 2
    return t


def _group_step(x, pos, b, wts, attend, mk, mv):
    n, d = x.shape
    t = n // b
    ckv, kr, kv, ik, ql, qr, dq, iq, iw = _mixer_in(x, pos, _tile(n, 512), wts)
    o_lat, o_dsa = attend(ql, qr, ckv, kr, dq, kv, iq, iw, ik)
    h = _mixer_out(o_lat, o_dsa, x, wts, _tile(n, 512))
    h = _cross(h, mk, mv, wts, b, _tile(t, 512))
    xn, e, gate = _peer_route(h, wts, _tile(n, 128), d // (2 * LANES))
    y = _peer_experts(xn, h, e, gate, wts, _tile(n, 64))
    return y, (ckv, kr, kv, ik)


def kernel(x_prompt, x_sample, cache_mla_ckv, cache_mla_kr, cache_dsa_kv, cache_dsa_idx, cache_mem_k, cache_mem_v, page_table, mem_prompt, norm_mix, w_in, mla_q_norm, w_uq, mla_kv_norm, w_uk, w_uv, w_o, norm_mem, mem_norm, wq_x, wk_x, wv_x, wo_x, norm_ffn, w_pq, peer_keys, peer_u, peer_v, norm_final):
    depth = w_in.shape[0]
    assert depth == 1, "one decoder layer"
    bp, tp, d = x_prompt.shape
    bs, ts, _ = x_sample.shape
    n_pool, page = cache_mla_ckv.shape[1], cache_mla_ckv.shape[2]
    assert page == PAGE
    past = page_table.shape[1] * page
    wts = _prep_weights(norm_mix[0], w_in[0], mla_q_norm[0], w_uq[0], mla_kv_norm[0], w_uk[0], w_uv[0], w_o[0],
                        norm_mem[0], mem_norm[0], wq_x[0], wk_x[0], wv_x[0], wo_x[0], norm_ffn[0], w_pq[0],
                        peer_keys[0], peer_u[0], peer_v[0], norm_final)
    k_prompt = min(DSA_TOPK, tp // 4)
    k_sample = min(DSA_TOPK, (past + ts) // 4)
    xw = X_HEADS * X_DIM
    n_mem = mem_prompt.shape[1]

    def attend_prompt(ql, qr, ckv, kr, dq, kv, iq, iw, ik):
        o_lat = _mla_prompt(ql, qr, ckv, kr, bp, tp, _tile(tp, 128), _tile(tp, 512))
        o_dsa = _dsa_prompt(iq, iw, ik, dq, kv, bp, tp, _tile(tp, 256), _tile(tp, 512), k_prompt)
        return o_lat, o_dsa

    g_pages = _tile(page_table.shape[1], 16)

    def attend_sample(ql, qr, ckv, kr, dq, kv, iq, iw, ik):
        o_lat = _mla_sample(page_table, ql, qr, ckv, kr, cache_mla_ckv[0], cache_mla_kr[0], bs, ts, g_pages)
        o_dsa = _dsa_sample(page_table, iq, iw, dq, ik, kv, cache_dsa_idx[0],
                            cache_dsa_kv[0].reshape(n_pool, page, -1), bs, ts, g_pages, k_sample)
        return o_lat, o_dsa

    mk_p, mv_p = _mem_kv(mem_prompt.reshape(bp * n_mem, d), wts, _tile(bp * n_mem, 512))
    y_p, (ckv_p, kr_p, kv_p, ik_p) = _group_step(
        x_prompt.reshape(bp * tp, d), jnp.arange(tp), bp, wts, attend_prompt,
        mk_p.reshape(bp, n_mem, xw), mv_p.reshape(bp, n_mem, xw))
    y_s, (ckv_s, kr_s, kv_s, ik_s) = _group_step(
        x_sample.reshape(bs * ts, d), past + jnp.arange(ts), bs, wts, attend_sample,
        cache_mem_k[0].reshape(bs, n_mem, xw), cache_mem_v[0].reshape(bs, n_mem, xw))
    kv_shape = (2, DSA_KV_HEADS, DSA_DIM)
    return (y_p.reshape(bp, tp, d), y_s.reshape(bs, ts, d),
            ckv_p.reshape(1, bp, tp, -1), kr_p.reshape(1, bp, tp, -1), kv_p.reshape((1, bp, tp) + kv_shape),
            ik_p.reshape(1, bp, tp, -1), mk_p.reshape(1, bp, n_mem, X_HEADS, X_DIM),
            mv_p.reshape(1, bp, n_mem, X_HEADS, X_DIM),
            ckv_s.reshape(1, bs, ts, -1), kr_s.reshape(1, bs, ts, -1), kv_s.reshape((1, bs, ts) + kv_shape),
            ik_s.reshape(1, bs, ts, -1))
```

```python
import functools
import math

import numpy as np
import jax
import jax.numpy as jnp
from jax import lax
from jax.experimental import pallas as pl
from jax.experimental.pallas import tpu as pltpu

F32 = jnp.float32
BF16 = jnp.bfloat16
I32 = jnp.int32
U32 = jnp.uint32

NORM_EPS = 1e-6
ROPE_THETA = 10000.0
PAGE = 128
LANES = 128
SUBLANES = 8
VMEM_LIMIT = 56 * 1024 * 1024
NEG = -1e30
INT_MIN = -2 ** 31

MLA_HEADS, MLA_NOPE, MLA_ROPE, MLA_RANK = 8, 64, 32, 128
DSA_HEADS, DSA_KV_HEADS, DSA_DIM = 8, 2, 64
IDX_HEADS, IDX_DIM = 8, 64
DSA_TOPK = 256
X_HEADS, X_DIM = 4, 64
PEER_HEADS, PEER_KEYS, PEER_TOPK = 8, 128, 16

C_CQ, C_CKV, C_KR, C_DQ, C_DK, C_DV, C_IQ, C_IK, C_IW, C_END = 0, 256, 384, 512, 1024, 1152, 1280, 1792, 1920, 2048


def _cparams(sem):
    return pltpu.CompilerParams(dimension_semantics=sem, vmem_limit_bytes=VMEM_LIMIT)


def _rms(x, g):
    return x * lax.rsqrt(jnp.mean(x * x, axis=-1, keepdims=True) + NORM_EPS) * g


def _dot(a, b):
    return jnp.dot(a.astype(BF16), b.astype(BF16), preferred_element_type=F32)


def _dot_nt(a, b):
    return lax.dot_general(a.astype(BF16), b.astype(BF16), (((1,), (1,)), ((), ())), preferred_element_type=F32)


def _rope(x, cos, sin, d):
    w = x.shape[-1]
    half = d // 2
    lane = lax.broadcasted_iota(I32, x.shape, 1)
    first = (lane & (d - 1)) < half
    fwd = pltpu.roll(x, w - half, axis=1)
    bwd = pltpu.roll(x, half, axis=1)
    return x * cos + jnp.where(first, fwd, bwd) * sin


def _rope_tables(pos, d, rows):
    half = d // 2
    inv = ROPE_THETA ** (-jnp.arange(half, dtype=F32) / half)
    ang = pos.astype(F32)[:, None] * inv[None, :]
    cos, sin = jnp.cos(ang), jnp.sin(ang)
    cos = jnp.tile(jnp.concatenate([cos, cos], -1), (1, LANES // d))
    sin = jnp.tile(jnp.concatenate([-sin, sin], -1), (1, LANES // d))
    reps = max(1, rows // pos.shape[0])
    return jnp.tile(cos, (reps, 1)), jnp.tile(sin, (reps, 1))


def _mix_in_body(x_ref, g_ref, w_ref, qn_ref, wuq_ref, kvn_ref, wuk_ref, c64_ref, s64_ref, c32_ref, s32_ref,
                 ckv_ref, kr_ref, kv_ref, ik_ref, ql_ref, qr_ref, dq_ref, iq_ref, iw_ref):
    xn = _rms(x_ref[...], g_ref[...])
    p = _dot(xn, w_ref[...])
    c64, s64, c32, s32 = c64_ref[...], s64_ref[...], c32_ref[...], s32_ref[...]
    c64x4 = jnp.concatenate([c64] * 4, axis=1)
    s64x4 = jnp.concatenate([s64] * 4, axis=1)
    q = _dot(_rms(p[:, C_CQ:C_CKV], qn_ref[...]), wuq_ref[...])
    nope = MLA_HEADS * MLA_NOPE
    for h in range(MLA_HEADS):
        ql_ref[h] = _dot(q[:, h * MLA_NOPE:(h + 1) * MLA_NOPE], wuk_ref[h])
    qr_ref[...] = _rope(q[:, nope:], jnp.concatenate([c32] * 2, axis=1), jnp.concatenate([s32] * 2, axis=1), MLA_ROPE)
    ckv_ref[...] = _rms(p[:, C_CKV:C_KR], kvn_ref[...])
    kr_ref[...] = _rope(p[:, C_KR:C_DQ], c32, s32, MLA_ROPE)[:, :MLA_ROPE]
    dq_ref[...] = _rope(p[:, C_DQ:C_DK], c64x4, s64x4, DSA_DIM)
    kv_ref[...] = jnp.concatenate([_rope(p[:, C_DK:C_DV], c64, s64, DSA_DIM), p[:, C_DV:C_IQ]], axis=1)
    iq_ref[...] = _rope(p[:, C_IQ:C_IK], c64x4, s64x4, IDX_DIM)
    ik_ref[...] = _rope(p[:, C_IK:C_IW], c64, s64, IDX_DIM)[:, :IDX_DIM]
    iw_ref[...] = p[:, C_IW:C_IW + IDX_HEADS]


def _mixer_in(x, pos, tm, wts):
    n, d = x.shape
    t = pos.shape[0]
    c64, s64 = _rope_tables(pos, DSA_DIM, tm)
    c32, s32 = _rope_tables(pos, MLA_ROPE, tm)
    tb = c64.shape[0] // tm
    row = lambda i: (i, 0)
    tab = lambda i: (i % tb, 0)
    full2 = lambda i: (0, 0)
    full3 = lambda i: (0, 0, 0)
    out_shapes = [
        jax.ShapeDtypeStruct((n, MLA_RANK), F32), jax.ShapeDtypeStruct((n, MLA_ROPE), F32),
        jax.ShapeDtypeStruct((n, 2 * DSA_KV_HEADS * DSA_DIM), F32), jax.ShapeDtypeStruct((n, IDX_DIM), F32),
        jax.ShapeDtypeStruct((MLA_HEADS, n, MLA_RANK), F32), jax.ShapeDtypeStruct((n, MLA_HEADS * MLA_ROPE), F32),
        jax.ShapeDtypeStruct((n, DSA_HEADS * DSA_DIM), F32), jax.ShapeDtypeStruct((n, IDX_HEADS * IDX_DIM), F32),
        jax.ShapeDtypeStruct((n, IDX_HEADS), F32)]
    out_specs = [
        pl.BlockSpec((tm, MLA_RANK), row), pl.BlockSpec((tm, MLA_ROPE), row),
        pl.BlockSpec((tm, 2 * DSA_KV_HEADS * DSA_DIM), row), pl.BlockSpec((tm, IDX_DIM), row),
        pl.BlockSpec((MLA_HEADS, tm, MLA_RANK), lambda i: (0, i, 0)), pl.BlockSpec((tm, MLA_HEADS * MLA_ROPE), row),
        pl.BlockSpec((tm, DSA_HEADS * DSA_DIM), row), pl.BlockSpec((tm, IDX_HEADS * IDX_DIM), row),
        pl.BlockSpec((tm, IDX_HEADS), row)]
    in_specs = [
        pl.BlockSpec((tm, d), row), pl.BlockSpec((1, d), full2), pl.BlockSpec((d, C_END), full2),
        pl.BlockSpec((1, 256), full2), pl.BlockSpec((256, MLA_HEADS * (MLA_NOPE + MLA_ROPE)), full2),
        pl.BlockSpec((1, MLA_RANK), full2), pl.BlockSpec((MLA_HEADS, MLA_NOPE, MLA_RANK), full3),
        pl.BlockSpec((tm, LANES), tab), pl.BlockSpec((tm, LANES), tab),
        pl.BlockSpec((tm, LANES), tab), pl.BlockSpec((tm, LANES), tab)]
    return pl.pallas_call(
        _mix_in_body, grid=(n // tm,), in_specs=in_specs, out_specs=out_specs, out_shape=out_shapes,
        compiler_params=_cparams(("parallel",)), name="mixer_in",
    )(x, wts["norm_mix"], wts["w_in"], wts["q_norm"], wts["w_uq"], wts["kv_norm"], wts["w_ukt"], c64, s64, c32, s32)


def _online_update(m_ref, l_ref, acc_ref, s, mask, v):
    if mask is not None:
        s = jnp.where(mask, s, NEG)
    m_old = m_ref[...]
    m_new = jnp.maximum(m_old, jnp.max(s, axis=1, keepdims=True))
    p = jnp.exp(s - m_new)
    alpha = jnp.exp(m_old - m_new)
    l_ref[...] = alpha * l_ref[...] + jnp.sum(p, axis=1, keepdims=True)
    acc_ref[...] = alpha * acc_ref[...] + _dot(p, v)
    m_ref[...] = m_new


def _flash_step(m_ref, acc_ref, s, v1, v_is_transposed=False):
    m_old = m_ref[...]
    m_new = jnp.maximum(m_old, jnp.max(s, axis=1, keepdims=True))
    p = jnp.exp(s - m_new)
    acc_ref[...] = jnp.exp(m_old - m_new) * acc_ref[...] + (_dot_nt(p, v1) if v_is_transposed else _dot(p, v1))
    m_ref[...] = m_new


def _with_ones(v, axis=1):
    return jnp.concatenate([v, jnp.ones(v.shape, v.dtype)], axis=axis)


def _mask_heads(s, mask, group):
    r, k = s.shape
    return jnp.where(mask[None], s.reshape(group, r // group, k), NEG).reshape(r, k)


def _init_state(m_ref, l_ref, acc_ref):
    m_ref[...] = jnp.full(m_ref.shape, NEG, F32)
    l_ref[...] = jnp.zeros(l_ref.shape, F32)
    acc_ref[...] = jnp.zeros(acc_ref.shape, F32)


def _stack_heads(x, n_heads, width):
    return jnp.concatenate([x[:, h * width:(h + 1) * width] for h in range(n_heads)], axis=0)


def _mla_prompt_body(ql_ref, qr_ref, ckv_ref, kr_ref, o_ref, m_ref, l_ref, acc_ref, *, tq, tk, sub):
    i = pl.program_id(1)
    rows = MLA_HEADS * tq
    scale = (MLA_NOPE + MLA_ROPE) ** -0.5
    q_l = (ql_ref[...].reshape(rows, MLA_RANK) * scale).astype(BF16)
    q_r = (_stack_heads(qr_ref[...], MLA_HEADS, MLA_ROPE) * scale).astype(BF16)
    _init_state(m_ref, l_ref, acc_ref)

    def block(j, masked):
        k0 = pl.multiple_of(j * tk, tk)
        kc = ckv_ref[pl.ds(k0, tk), :].astype(BF16)
        kr = kr_ref[pl.ds(k0, tk), :].astype(BF16)
        for r0 in range(0, rows, sub):
            s = _dot_nt(q_l[r0:r0 + sub], kc) + _dot_nt(q_r[r0:r0 + sub], kr)
            mask = None
            if masked:
                row_t = i * tq + ((r0 + lax.broadcasted_iota(I32, (sub, 1), 0)) & (tq - 1))
                mask = (k0 + lax.broadcasted_iota(I32, (sub, tk), 1)) <= row_t
            sl = pl.ds(r0, sub)
            _online_update(m_ref.at[sl], l_ref.at[sl], acc_ref.at[sl], s, mask, kc)

    n_full = (i * tq + 1) // tk

    def step(j, carry):
        block(j, False)
        return carry

    lax.fori_loop(0, n_full, step, 0)
    block(n_full, True)
    o_ref[...] = (acc_ref[...] / l_ref[...]).reshape(MLA_HEADS, tq, MLA_RANK)


def _mla_prompt(ql, qr, ckv, kr, b, t, tq, tk):
    n = b * t
    nq = t // tq
    rows = MLA_HEADS * tq
    assert tk % tq == 0
    return pl.pallas_call(
        functools.partial(_mla_prompt_body, tq=tq, tk=tk, sub=min(rows, 256)),
        grid=(b, nq),
        in_specs=[pl.BlockSpec((MLA_HEADS, tq, MLA_RANK), lambda bi, i: (0, bi * nq + i, 0)),
                  pl.BlockSpec((tq, MLA_HEADS * MLA_ROPE), lambda bi, i: (bi * nq + i, 0)),
                  pl.BlockSpec((t, MLA_RANK), lambda bi, i: (bi, 0)),
                  pl.BlockSpec((t, MLA_ROPE), lambda bi, i: (bi, 0))],
        out_specs=pl.BlockSpec((MLA_HEADS, tq, MLA_RANK), lambda bi, i: (0, bi * nq + i, 0)),
        out_shape=jax.ShapeDtypeStruct((MLA_HEADS, n, MLA_RANK), F32),
        scratch_shapes=[pltpu.VMEM((rows, 1), F32), pltpu.VMEM((rows, 1), F32), pltpu.VMEM((rows, MLA_RANK), F32)],
        compiler_params=_cparams(("parallel", "arbitrary")), name="mla_prompt",
    )(ql, qr, ckv, kr)


def _order_key(score):
    bits = lax.bitcast_convert_type(score + 0.0, I32)
    return bits ^ ((bits >> 31) & 0x7FFFFFFF)


def _kth_largest(count_ge, shape, k):
    kf = float(k)
    tau0 = jnp.where(count_ge(jnp.zeros(shape, I32)) >= kf, 0, INT_MIN).astype(I32)

    def bit(it, tau):
        cand = tau | lax.shift_left(jnp.int32(1), 30 - it)
        return jnp.where(count_ge(cand) >= kf, cand, tau)

    return lax.fori_loop(0, 31, bit, tau0)


def _select_mask(key, valid, tau, need, carry, tri):
    masks = []
    for c in range(key.shape[1] // LANES):
        kb = key[:, c * LANES:(c + 1) * LANES]
        eq, gt = kb == tau, kb > tau
        if valid is not None:
            vb = valid[:, c * LANES:(c + 1) * LANES]
            eq, gt = eq & vb, gt & vb
        eqf = jnp.where(eq, 1.0, 0.0)
        before = carry + jnp.dot(eqf.astype(BF16), tri, preferred_element_type=F32)
        masks.append(gt | (eq & (before < need)))
        carry = carry + jnp.sum(eqf, axis=1, keepdims=True)
    return jnp.concatenate(masks, axis=1), carry


def _strict_upper(n):
    r = lax.broadcasted_iota(I32, (n, n), 0)
    c = lax.broadcasted_iota(I32, (n, n), 1)
    return jnp.where(r < c, 1.0, 0.0).astype(BF16)


def _index_scores(iq_s, iw_b, ikc, keys_transposed=False):
    t = iq_s.shape[0] // IDX_HEADS
    d = _dot(iq_s, ikc) if keys_transposed else _dot_nt(iq_s, ikc)
    score = None
    for h in range(IDX_HEADS):
        term = iw_b(h) * jnp.maximum(d[h * t:(h + 1) * t], 0.0)
        score = term if score is None else score + term
    return score


def _count_lanes(m):
    part = m[:, :LANES]
    for c in range(1, m.shape[1] // LANES):
        part = part + m[:, c * LANES:(c + 1) * LANES]
    return part


def _dsa_prompt_body(iq_ref, iw_ref, ik_ref, dq_ref, kv_ref, o_ref, key_ref, iwb_ref, m_ref, acc_ref, *, tq, ck, topk):
    i = pl.program_id(1)
    nc = ((i + 1) * tq + ck - 1) // ck
    group = DSA_HEADS // DSA_KV_HEADS
    iq_s = _stack_heads(iq_ref[...], IDX_HEADS, IDX_DIM).astype(BF16)
    iw = iw_ref[...]
    for h in range(IDX_HEADS):
        iwb_ref[h] = jnp.broadcast_to(iw[:, h:h + 1], (tq, LANES))
    iw_b = lambda h: jnp.concatenate([iwb_ref[h]] * (ck // LANES), axis=1)
    q_t = i * tq + lax.broadcasted_iota(I32, (tq, 1), 0)

    def causal(c):
        return (c * ck + lax.broadcasted_iota(I32, (tq, ck), 1)) <= q_t

    def score_chunk(c, carry):
        k0 = pl.multiple_of(c * ck, ck)
        score = _index_scores(iq_s, iw_b, ik_ref[pl.ds(k0, ck), :].astype(BF16))
        key_ref[c] = jnp.where(causal(c), _order_key(score), INT_MIN)
        return carry

    lax.fori_loop(0, nc, score_chunk, 0)

    def count(pred):
        def body(c, acc):
            return acc + _count_lanes(jnp.where(pred(key_ref[c]), 1.0, 0.0))
        return jnp.sum(lax.fori_loop(0, nc, body, jnp.zeros((tq, LANES), F32)), axis=1, keepdims=True)

    tau = _kth_largest(lambda cand: count(lambda k: k >= cand), (tq, 1), topk)
    need = float(topk) - count(lambda k: k > tau)
    tri = _strict_upper(LANES)
    dq = dq_ref[...] * DSA_DIM ** -0.5
    qs = [_stack_heads(dq[:, g * group * DSA_DIM:(g + 1) * group * DSA_DIM], group, DSA_DIM).astype(BF16)
          for g in range(DSA_KV_HEADS)]
    m_ref[...] = jnp.full(m_ref.shape, NEG, F32)
    acc_ref[...] = jnp.zeros(acc_ref.shape, F32)

    def attend(c, carry):
        k0 = pl.multiple_of(c * ck, ck)
        mask, carry = _select_mask(key_ref[c], causal(c), tau, need, carry, tri)
        kvc = kv_ref[pl.ds(k0, ck), :].astype(BF16)
        for g in range(DSA_KV_HEADS):
            kc = kvc[:, g * DSA_DIM:(g + 1) * DSA_DIM]
            v1 = _with_ones(kvc[:, (DSA_KV_HEADS + g) * DSA_DIM:(DSA_KV_HEADS + g + 1) * DSA_DIM])
            for n in range(group):
                sl = pl.ds(n * tq, tq)
                s = jnp.where(mask, _dot_nt(qs[g][n * tq:(n + 1) * tq], kc), NEG)
                _flash_step(m_ref.at[g, sl], acc_ref.at[g, sl], s, v1)
        return carry

    lax.fori_loop(0, nc, attend, jnp.zeros((tq, 1), F32))
    outs = []
    for g in range(DSA_KV_HEADS):
        acc = acc_ref[g]
        o = acc[:, :DSA_DIM] / acc[:, DSA_DIM:DSA_DIM + 1]
        outs += [o[n * tq:(n + 1) * tq] for n in range(group)]
    o_ref[...] = jnp.concatenate(outs, axis=1)


def _dsa_prompt(iq, iw, ik, dq, kv, b, t, tq, ck, topk):
    n = b * t
    nq = t // tq
    group = DSA_HEADS // DSA_KV_HEADS
    rows = group * tq
    qrow = lambda bi, i: (bi * nq + i, 0)
    brow = lambda bi, i: (bi, 0)
    return pl.pallas_call(
        functools.partial(_dsa_prompt_body, tq=tq, ck=ck, topk=topk),
        grid=(b, nq),
        in_specs=[pl.BlockSpec((tq, IDX_HEADS * IDX_DIM), qrow), pl.BlockSpec((tq, IDX_HEADS), qrow),
                  pl.BlockSpec((t, IDX_DIM), brow), pl.BlockSpec((tq, DSA_HEADS * DSA_DIM), qrow),
                  pl.BlockSpec((t, 2 * DSA_KV_HEADS * DSA_DIM), brow)],
        out_specs=pl.BlockSpec((tq, DSA_HEADS * DSA_DIM), qrow),
        out_shape=jax.ShapeDtypeStruct((n, DSA_HEADS * DSA_DIM), F32),
        scratch_shapes=[pltpu.VMEM((t // ck, tq, ck), I32), pltpu.VMEM((IDX_HEADS, tq, LANES), F32),
                        pltpu.VMEM((DSA_KV_HEADS, rows, 1), F32), pltpu.VMEM((DSA_KV_HEADS, rows, 2 * DSA_DIM), F32)],
        compiler_params=_cparams(("parallel", "arbitrary")), name="dsa_prompt",
    )(iq, iw, ik, dq, kv)


def _mla_sample_body(pt_ref, ql_ref, qr_ref, ckvn_ref, krn_ref, *rest, t, g_pages):
    ckv_pages = rest[:g_pages]
    kr_pages = rest[g_pages:2 * g_pages]
    o_ref, m_ref, l_ref, acc_ref = rest[2 * g_pages:]
    j = pl.program_id(1)
    rows = MLA_HEADS * t
    scale = (MLA_NOPE + MLA_ROPE) ** -0.5
    q_l = (ql_ref[...].reshape(rows, MLA_RANK) * scale).astype(BF16)
    q_r = (_stack_heads(qr_ref[...], MLA_HEADS, MLA_ROPE) * scale).astype(BF16)

    @pl.when(j == 0)
    def _():
        _init_state(m_ref, l_ref, acc_ref)

    kc = jnp.concatenate([r[0] for r in ckv_pages], axis=0).astype(BF16)
    kr_t = jnp.concatenate([r[0] for r in kr_pages], axis=1).astype(BF16)
    _online_update(m_ref, l_ref, acc_ref, _dot_nt(q_l, kc) + _dot(q_r, kr_t), None, kc)

    @pl.when(j == pl.num_programs(1) - 1)
    def _():
        kn = ckvn_ref[0].astype(BF16)
        s = _dot_nt(q_l, kn) + _dot_nt(q_r, krn_ref[0].astype(BF16))
        row_t = lax.broadcasted_iota(I32, (rows, PAGE), 0) & (t - 1)
        _online_update(m_ref, l_ref, acc_ref, s, lax.broadcasted_iota(I32, (rows, PAGE), 1) <= row_t, kn)
        o_ref[...] = (acc_ref[...] / l_ref[...]).reshape(MLA_HEADS, t, MLA_RANK)


def _pad_new(x, b, t):
    return jnp.pad(x.reshape(b, t, x.shape[-1]), ((0, 0), (0, PAGE - t), (0, 0)))


def _mla_sample(page_table, ql, qr, ckv_new, kr_new, pool_ckv, pool_kr_t, b, t, g_pages):
    n_pages = page_table.shape[1]
    steps = n_pages // g_pages
    rows = MLA_HEADS * t
    page = lambda g: (lambda bi, j, pt: (pt[bi, j * g_pages + g], 0, 0))
    seq3 = lambda bi, j, pt: (bi, 0, 0)
    in_specs = [pl.BlockSpec((MLA_HEADS, t, MLA_RANK), lambda bi, j, pt: (0, bi, 0)),
                pl.BlockSpec((t, MLA_HEADS * MLA_ROPE), lambda bi, j, pt: (bi, 0)),
                pl.BlockSpec((1, PAGE, MLA_RANK), seq3), pl.BlockSpec((1, PAGE, MLA_ROPE), seq3)]
    in_specs += [pl.BlockSpec((1, PAGE, MLA_RANK), page(g)) for g in range(g_pages)]
    in_specs += [pl.BlockSpec((1, MLA_ROPE, PAGE), page(g)) for g in range(g_pages)]
    grid_spec = pltpu.PrefetchScalarGridSpec(
        num_scalar_prefetch=1, grid=(b, steps), in_specs=in_specs,
        out_specs=pl.BlockSpec((MLA_HEADS, t, MLA_RANK), lambda bi, j, pt: (0, bi, 0)),
        scratch_shapes=[pltpu.VMEM((rows, 1), F32), pltpu.VMEM((rows, 1), F32), pltpu.VMEM((rows, MLA_RANK), F32)])
    return pl.pallas_call(
        functools.partial(_mla_sample_body, t=t, g_pages=g_pages), grid_spec=grid_spec,
        out_shape=jax.ShapeDtypeStruct((MLA_HEADS, b * t, MLA_RANK), F32),
        compiler_params=_cparams(("parallel", "arbitrary")), name="mla_sample",
    )(page_table, ql, qr, _pad_new(ckv_new, b, t), _pad_new(kr_new, b, t),
      *([pool_ckv] * g_pages), *([pool_kr_t] * g_pages))


def _dsa_sample_body(pt_ref, iq_ref, iw_ref, dq_ref, ikn_ref, kvn_ref, *rest, t, g_pages, steps, topk):
    ik_pages = rest[:g_pages]
    kv_pages = rest[g_pages:2 * g_pages]
    o_ref, key_ref, keyn_ref, tau_ref, need_ref, carry_ref, m_ref, acc_ref = rest[2 * g_pages:]
    j = pl.program_id(1)
    ck = g_pages * PAGE
    group = DSA_HEADS // DSA_KV_HEADS
    scale = DSA_DIM ** -0.5
    new_valid = lax.broadcasted_iota(I32, (t, PAGE), 1) <= lax.broadcasted_iota(I32, (t, PAGE), 0)

    @pl.when(j < steps)
    def _():
        iq_s = _stack_heads(iq_ref[...], IDX_HEADS, IDX_DIM).astype(BF16)
        iw = iw_ref[...]
        ik_t = jnp.concatenate([r[0] for r in ik_pages], axis=1).astype(BF16)
        key_ref[j] = _order_key(_index_scores(iq_s, lambda h: jnp.broadcast_to(iw[:, h:h + 1], (t, ck)), ik_t, True))

        @pl.when(j == steps - 1)
        def _():
            sn = _index_scores(iq_s, lambda h: jnp.broadcast_to(iw[:, h:h + 1], (t, PAGE)), ikn_ref[0].astype(BF16))
            keyn_ref[...] = jnp.where(new_valid, _order_key(sn), INT_MIN)

    @pl.when(j == steps)
    def _():
        def count(pred):
            acc = jnp.where(pred(keyn_ref[...]), 1.0, 0.0)
            for c in range(steps):
                acc = acc + _count_lanes(jnp.where(pred(key_ref[c]), 1.0, 0.0))
            return jnp.sum(acc, axis=1, keepdims=True)

        tau = _kth_largest(lambda cand: count(lambda k: k >= cand), (t, 1), topk)
        tau_ref[...] = tau
        need_ref[...] = float(topk) - count(lambda k: k > tau)
        carry_ref[...] = jnp.zeros((t, 1), F32)
        m_ref[...] = jnp.full(m_ref.shape, NEG, F32)
        acc_ref[...] = jnp.zeros(acc_ref.shape, F32)

    def attend(key, valid, kvc, transposed):
        mask, carry = _select_mask(key, valid, tau_ref[...], need_ref[...], carry_ref[...], _strict_upper(LANES))
        carry_ref[...] = carry
        dq = dq_ref[...] * scale
        for g in range(DSA_KV_HEADS):
            qs = _stack_heads(dq[:, g * group * DSA_DIM:(g + 1) * group * DSA_DIM], group, DSA_DIM)
            ks = slice(g * DSA_DIM, (g + 1) * DSA_DIM)
            vs = slice((DSA_KV_HEADS + g) * DSA_DIM, (DSA_KV_HEADS + g + 1) * DSA_DIM)
            if transposed:
                s, v1 = _dot(qs, kvc[ks]), _with_ones(kvc[vs], axis=0)
            else:
                s, v1 = _dot_nt(qs, kvc[:, ks]), _with_ones(kvc[:, vs])
            _flash_step(m_ref.at[g], acc_ref.at[g], _mask_heads(s, mask, group), v1, transposed)

    @pl.when(j >= steps)
    def _():
        kv_t = jnp.concatenate([r[0] for r in kv_pages], axis=1).astype(BF16)
        attend(key_ref[j - steps], None, kv_t, True)

        @pl.when(j == 2 * steps - 1)
        def _():
            attend(keyn_ref[...], new_valid, kvn_ref[0].astype(BF16), False)
            outs = []
            for g in range(DSA_KV_HEADS):
                acc = acc_ref[g]
                o = acc[:, :DSA_DIM] / acc[:, DSA_DIM:DSA_DIM + 1]
                outs += [o[n * t:(n + 1) * t] for n in range(group)]
            o_ref[...] = jnp.concatenate(outs, axis=1)


def _dsa_sample(page_table, iq, iw, dq, ik_new, kv_new, pool_ik_t, pool_kv_t, b, t, g_pages, topk):
    n_pages = page_table.shape[1]
    steps = n_pages // g_pages
    group = DSA_HEADS // DSA_KV_HEADS
    rows = group * t
    kvw = 2 * DSA_KV_HEADS * DSA_DIM
    ik_page = lambda g: (lambda bi, j, pt: (pt[bi, jnp.minimum(j, steps - 1) * g_pages + g], 0, 0))
    kv_page = lambda g: (lambda bi, j, pt: (pt[bi, jnp.maximum(j - steps, 0) * g_pages + g], 0, 0))
    seq2 = lambda bi, j, pt: (bi, 0)
    seq3 = lambda bi, j, pt: (bi, 0, 0)
    in_specs = [pl.BlockSpec((t, IDX_HEADS * IDX_DIM), seq2), pl.BlockSpec((t, IDX_HEADS), seq2),
                pl.BlockSpec((t, DSA_HEADS * DSA_DIM), seq2),
                pl.BlockSpec((1, PAGE, IDX_DIM), seq3), pl.BlockSpec((1, PAGE, kvw), seq3)]
    in_specs += [pl.BlockSpec((1, IDX_DIM, PAGE), ik_page(g)) for g in range(g_pages)]
    in_specs += [pl.BlockSpec((1, kvw, PAGE), kv_page(g)) for g in range(g_pages)]
    grid_spec = pltpu.PrefetchScalarGridSpec(
        num_scalar_prefetch=1, grid=(b, 2 * steps), in_specs=in_specs,
        out_specs=pl.BlockSpec((t, DSA_HEADS * DSA_DIM), seq2),
        scratch_shapes=[pltpu.VMEM((steps, t, g_pages * PAGE), I32), pltpu.VMEM((t, PAGE), I32),
                        pltpu.VMEM((t, 1), I32), pltpu.VMEM((t, 1), F32), pltpu.VMEM((t, 1), F32),
                        pltpu.VMEM((DSA_KV_HEADS, rows, 1), F32), pltpu.VMEM((DSA_KV_HEADS, rows, 2 * DSA_DIM), F32)])
    return pl.pallas_call(
        functools.partial(_dsa_sample_body, t=t, g_pages=g_pages, steps=steps, topk=topk), grid_spec=grid_spec,
        out_shape=jax.ShapeDtypeStruct((b * t, DSA_HEADS * DSA_DIM), F32),
        compiler_params=_cparams(("parallel", "arbitrary")), name="dsa_sample",
    )(page_table, iq, iw, dq, _pad_new(ik_new, b, t), _pad_new(kv_new, b, t),
      *([pool_ik_t] * g_pages), *([pool_kv_t] * g_pages))


def _mix_out_body(ol_ref, od_ref, h_ref, wuv_ref, wo_ref, o_ref):
    mla = jnp.concatenate([_dot(ol_ref[h], wuv_ref[h]) for h in range(MLA_HEADS)], axis=1)
    half = mla.shape[1]
    o_ref[...] = h_ref[...] + _dot(mla, wo_ref[:half, :]) + _dot(od_ref[...], wo_ref[half:, :])


def _mixer_out(o_lat, o_dsa, h, wts, tm):
    n, d = h.shape
    row = lambda i: (i, 0)
    v_dim = wts["w_uvh"].shape[-1]
    mix = wts["w_o"].shape[0]
    return pl.pallas_call(
        _mix_out_body, grid=(n // tm,),
        in_specs=[pl.BlockSpec((MLA_HEADS, tm, MLA_RANK), lambda i: (0, i, 0)), pl.BlockSpec((tm, DSA_HEADS * DSA_DIM), row),
                  pl.BlockSpec((tm, d), row), pl.BlockSpec((MLA_HEADS, MLA_RANK, v_dim), lambda i: (0, 0, 0)),
                  pl.BlockSpec((mix, d), lambda i: (0, 0))],
        out_specs=pl.BlockSpec((tm, d), row), out_shape=jax.ShapeDtypeStruct((n, d), F32),
        compiler_params=_cparams(("parallel",)), name="mixer_out",
    )(o_lat, o_dsa, h, wts["w_uvh"], wts["w_o"])


def _mem_kv_body(m_ref, g_ref, w_ref, k_ref, v_ref):
    p = _dot(_rms(m_ref[...], g_ref[...]), w_ref[...])
    half = p.shape[1] // 2
    k_ref[...] = p[:, :half]
    v_ref[...] = p[:, half:]


def _mem_kv(mem, wts, tm):
    n, d = mem.shape
    w = wts["w_kvx"]
    half = w.shape[1] // 2
    row = lambda i: (i, 0)
    return pl.pallas_call(
        _mem_kv_body, grid=(n // tm,),
        in_specs=[pl.BlockSpec((tm, d), row), pl.BlockSpec((1, d), lambda i: (0, 0)), pl.BlockSpec(w.shape, lambda i: (0, 0))],
        out_specs=[pl.BlockSpec((tm, half), row)] * 2, out_shape=[jax.ShapeDtypeStruct((n, half), F32)] * 2,
        compiler_params=_cparams(("parallel",)), name="mem_kv",
    )(mem, wts["mem_norm"], w)


def _cross_body(h_ref, g_ref, wq_ref, k_ref, v_ref, wo_ref, o_ref):
    h = h_ref[...]
    q = _dot(_rms(h, g_ref[...]), wq_ref[...])
    k = k_ref[0].astype(BF16)
    v = v_ref[0].astype(BF16)
    outs = []
    for hd in range(X_HEADS):
        sl = slice(hd * X_DIM, (hd + 1) * X_DIM)
        s = _dot_nt(q[:, sl], k[:, sl]) * X_DIM ** -0.5
        e = jnp.exp(s - jnp.max(s, axis=1, keepdims=True))
        p = e / jnp.sum(e, axis=1, keepdims=True)
        outs.append(_dot(p, v[:, sl]))
    o_ref[...] = h + _dot(jnp.concatenate(outs, axis=1), wo_ref[...])


def _cross(h, mk, mv, wts, b, tm):
    n, d = h.shape
    per = (n // b) // tm
    m, w = mk.shape[1], mk.shape[2]
    row = lambda i: (i, 0)
    mem = lambda i: (i // per, 0, 0)
    return pl.pallas_call(
        _cross_body, grid=(n // tm,),
        in_specs=[pl.BlockSpec((tm, d), row), pl.BlockSpec((1, d), lambda i: (0, 0)), pl.BlockSpec((d, w), lambda i: (0, 0)),
                  pl.BlockSpec((1, m, w), mem), pl.BlockSpec((1, m, w), mem), pl.BlockSpec((w, d), lambda i: (0, 0))],
        out_specs=pl.BlockSpec((tm, d), row), out_shape=jax.ShapeDtypeStruct((n, d), F32),
        compiler_params=_cparams(("parallel",)), name="cross_attn",
    )(h, wts["norm_mem"], wts["wq_x"], mk, mv, wts["wo_x"])


def _extract_topk(s, payload, k):
    r = s.shape[0]
    row = lax.broadcasted_iota(I32, s.shape, 0)
    vals, pays = [], []
    for _ in range(k):
        m = jnp.max(s, axis=0, keepdims=True)
        first = jnp.min(jnp.where(s == m, row, r), axis=0, keepdims=True)
        hit = row == first
        vals.append(m)
        pays.append(first if payload is None else jnp.sum(jnp.where(hit, payload, 0), axis=0, keepdims=True))
        s = jnp.where(hit, -jnp.inf, s)
    return jnp.concatenate(vals, axis=0), jnp.concatenate(pays, axis=0)


def _pair_blocks(k):
    blocks = []
    a = 0
    while a < k and k // (a + 1) > 1:
        nb = k // (a + 1)
        blocks.append((a, nb, -(-nb // SUBLANES) * SUBLANES))
        a += 1
    return blocks, a


ROUTE_HEADS_PER_ITER = 2


def _peer_route_body(h_ref, g_ref, wpq_ref, keys_ref, xn_ref, e_ref, gate_ref, q_scr, gs_scr, ge_scr, *, row_words):
    tm = h_ref.shape[0]
    k = PEER_TOPK
    xn = _rms(h_ref[...], g_ref[...])
    xn_ref[...] = xn
    q = _dot(xn, wpq_ref[...])
    n_sub = 2 * PEER_HEADS
    kd = q.shape[1] // n_sub
    for hp in range(n_sub):
        q_scr[hp] = q[:, hp * kd:(hp + 1) * kd].astype(BF16)
    blocks, a_tail = _pair_blocks(k)

    def route_head(h):
        (t1, i1), (t2, i2) = [_extract_topk(_dot_nt(keys_ref[2 * h + p], q_scr[2 * h + p]), None, k) for p in range(2)]
        cand, ids = [], []
        for a, nb, rows in blocks:
            live = lax.broadcasted_iota(I32, (rows, tm), 0) < nb
            cand.append(jnp.where(live, t1[a:a + 1] + t2[:rows], -jnp.inf))
            ids.append(i1[a:a + 1] * PEER_KEYS + i2[:rows])
        cand.append(t1[a_tail:] + t2[0:1])
        ids.append(i1[a_tail:] * PEER_KEYS + i2[0:1])
        gs_scr[h], ge_scr[h] = _extract_topk(jnp.concatenate(cand, axis=0), jnp.concatenate(ids, axis=0), k)

    def heads(it, carry):
        for j in range(ROUTE_HEADS_PER_ITER):
            route_head(it * ROUTE_HEADS_PER_ITER + j)
        return carry

    lax.fori_loop(0, PEER_HEADS // ROUTE_HEADS_PER_ITER, heads, 0)
    gs = gs_scr[...]
    e = jnp.exp(gs - jnp.max(gs, axis=1, keepdims=True))
    gate = e / jnp.sum(e, axis=1, keepdims=True)
    gate_ref[...] = gate.reshape(PEER_HEADS * k, tm).T
    e_ref[...] = (ge_scr[...] * row_words).reshape(PEER_HEADS * k, tm).T


def _peer_route(h, wts, tm, row_words):
    n, d = h.shape
    slots = PEER_HEADS * PEER_TOPK
    keys = wts["peer_keys"]
    n_sub, n_keys, kd = keys.shape
    row = lambda i: (i, 0)
    return pl.pallas_call(
        functools.partial(_peer_route_body, row_words=row_words), grid=(n // tm,),
        in_specs=[pl.BlockSpec((tm, d), row), pl.BlockSpec((1, d), lambda i: (0, 0)),
                  pl.BlockSpec((d, n_sub * kd), lambda i: (0, 0)), pl.BlockSpec(keys.shape, lambda i: (0, 0, 0))],
        out_specs=[pl.BlockSpec((tm, d), row), pl.BlockSpec((tm, slots), row), pl.BlockSpec((tm, slots), row)],
        out_shape=[jax.ShapeDtypeStruct((n, d), F32), jax.ShapeDtypeStruct((n, slots), I32),
                   jax.ShapeDtypeStruct((n, slots), F32)],
        scratch_shapes=[pltpu.VMEM((n_sub, tm, kd), BF16), pltpu.VMEM((PEER_HEADS, PEER_TOPK, tm), F32),
                        pltpu.VMEM((PEER_HEADS, PEER_TOPK, tm), I32)],
        compiler_params=_cparams(("parallel",)), name="peer_route",
    )(h, wts["norm_ffn"], wts["w_pq"], keys)


def _pack_table(w):
    e, d = w.shape
    bits = lax.bitcast_convert_type(w.astype(jnp.bfloat16), jnp.uint16).astype(U32)
    packed = bits[:, :d // 2] | (bits[:, d // 2:] << 16)
    return packed.reshape(e * d // (2 * LANES), LANES)


def _unpack_row(tbl_ref, e_row, rows):
    w = tbl_ref[pl.ds(pl.multiple_of(e_row, rows), rows), :]
    lo = lax.bitcast_convert_type(w << 16, F32)
    hi = lax.bitcast_convert_type(w & jnp.uint32(0xFFFF0000), F32)
    return lo, hi


def _load_table(tbl_hbm, tbl_ref, sem):
    @pl.when(pl.program_id(0) == 0)
    def _():
        cp = pltpu.make_async_copy(tbl_hbm, tbl_ref, sem)
        cp.start()
        cp.wait()


TOKENS_PER_ITER = 2


def _peer_up_body(e_ref, x_ref, gate_ref, tbl_hbm, a_ref, tbl_ref, p_ref, t_ref, sem, *, tb, slots, rows):
    _load_table(tbl_hbm, tbl_ref, sem)

    def tokens(it, carry):
        for j in range(TOKENS_PER_ITER):
            n = it * TOKENS_PER_ITER + j
            xv = x_ref[pl.ds(pl.multiple_of(n * 2 * rows, 2 * rows), 2 * rows), :]
            xa, xb = xv[:rows], xv[rows:]
            for k in range(slots):
                e_grp = e_ref.at[n, pl.ds((k // 8) * 8, 8)]
                lo, hi = _unpack_row(tbl_ref, e_grp[k % 8], rows)
                p_ref[j, pl.ds(k * rows, rows), :] = lo * xa + hi * xb
            t = p_ref[j, pl.ds(0, slots, stride=rows), :]
            for r in range(1, rows):
                t = t + p_ref[j, pl.ds(r, slots, stride=rows), :]
            t_ref[n] = t
        return carry

    lax.fori_loop(0, tb // TOKENS_PER_ITER, tokens, 0)
    for n in range(tb):
        hrow = jnp.sum(t_ref[n].T, axis=0, keepdims=True)
        act = 0.5 * hrow * (1.0 + lax.erf(hrow * (2.0 ** -0.5)))
        a_ref[n] = gate_ref[n] * act


def _peer_down_body(e_ref, a_ref, h_ref, g_ref, tbl_hbm, y_ref, tbl_ref, s_ref, sem, *, tb, slots, rows):
    _load_table(tbl_hbm, tbl_ref, sem)
    d = 2 * rows * LANES
    for n in range(tb):
        s_ref[n] = jnp.broadcast_to(a_ref[n], (slots, slots)).T

    def tokens(it, carry):
        for j in range(TOKENS_PER_ITER):
            n = it * TOKENS_PER_ITER + j
            acc = [jnp.zeros((rows, LANES), F32) for _ in range(4)]
            for k in range(slots):
                e_grp = e_ref.at[n, pl.ds((k // 8) * 8, 8)]
                lo, hi = _unpack_row(tbl_ref, e_grp[k % 8], rows)
                a = jnp.broadcast_to(s_ref[n, pl.ds(k, 1), :], (rows, LANES))
                acc[2 * (k % 2)] = acc[2 * (k % 2)] + a * lo
                acc[2 * (k % 2) + 1] = acc[2 * (k % 2) + 1] + a * hi
            r0 = pl.multiple_of(n * 2 * rows, 2 * rows)
            y_ref[pl.ds(r0, 2 * rows), :] = (h_ref[pl.ds(r0, 2 * rows), :]
                                             + jnp.concatenate([acc[0] + acc[2], acc[1] + acc[3]], axis=0))
        return carry

    lax.fori_loop(0, tb // TOKENS_PER_ITER, tokens, 0)
    h = y_ref[...].reshape(tb, 2 * rows, LANES)
    ss = jnp.sum(jnp.sum(h * h, axis=2, keepdims=True), axis=1, keepdims=True)
    y_ref[...] = (h * lax.rsqrt(ss / d + NORM_EPS) * g_ref[...][None]).reshape(tb * 2 * rows, LANES)


def _peer_experts(xn, h, e, gate, wts, tb):
    n, d = h.shape
    slots = e.shape[1]
    assert slots == LANES
    rows = d // (2 * LANES)
    vr = 2 * rows
    tok = lambda i: (i, 0)
    smem = functools.partial(pl.BlockSpec, memory_space=pltpu.SMEM)
    tbl_shape = wts["peer_u"].shape
    common = dict(grid=(n // tb,), compiler_params=_cparams(("arbitrary",)))
    tok3 = lambda i: (i, 0, 0)
    a = pl.pallas_call(
        functools.partial(_peer_up_body, tb=tb, slots=slots, rows=rows),
        in_specs=[smem((tb, slots), tok), pl.BlockSpec((tb * vr, LANES), tok), pl.BlockSpec((tb, 1, slots), tok3),
                  pl.BlockSpec(memory_space=pl.ANY)],
        out_specs=pl.BlockSpec((tb, 1, slots), tok3), out_shape=jax.ShapeDtypeStruct((n, 1, slots), F32),
        scratch_shapes=[pltpu.VMEM(tbl_shape, U32), pltpu.VMEM((TOKENS_PER_ITER, slots * rows, LANES), F32),
                        pltpu.VMEM((tb, slots, LANES), F32), pltpu.SemaphoreType.DMA(())],
        name="peer_up", **common,
    )(e, xn.reshape(n * vr, LANES), gate.reshape(n, 1, slots), wts["peer_u"])
    y = pl.pallas_call(
        functools.partial(_peer_down_body, tb=tb, slots=slots, rows=rows),
        in_specs=[smem((tb, slots), tok), pl.BlockSpec((tb, 1, slots), tok3), pl.BlockSpec((tb * vr, LANES), tok),
                  pl.BlockSpec((vr, LANES), lambda i: (0, 0)), pl.BlockSpec(memory_space=pl.ANY)],
        out_specs=pl.BlockSpec((tb * vr, LANES), tok), out_shape=jax.ShapeDtypeStruct((n * vr, LANES), F32),
        scratch_shapes=[pltpu.VMEM(tbl_shape, U32), pltpu.VMEM((tb, slots, LANES), F32), pltpu.SemaphoreType.DMA(())],
        name="peer_down", **common,
    )(e, a, h.reshape(n * vr, LANES), wts["norm_final"].reshape(vr, LANES), wts["peer_v"])
    return y.reshape(n, d)


def _prep_weights(norm_mix, w_in, mla_q_norm, w_uq, mla_kv_norm, w_uk, w_uv, w_o, norm_mem, mem_norm, wq_x, wk_x,
                  wv_x, wo_x, norm_ffn, w_pq, peer_keys, peer_u, peer_v, norm_final):
    d = w_in.shape[0]
    widths = (256, 128, 32, 512, 128, 128, 512, 8, 64)
    offs = np.concatenate([[0], np.cumsum(widths)])
    cq, ckv, kr, dq, dk, dv, iq, iw, ik = [w_in[:, offs[i]:offs[i + 1]] for i in range(9)]
    pad = lambda w, to: jnp.pad(w, ((0, 0), (0, to - w.shape[1])))
    fused = jnp.concatenate([cq, ckv, pad(kr, 128), dq, dk, dv, iq, pad(ik, 128), pad(iw, 128)], axis=1)
    hd = MLA_NOPE + MLA_ROPE
    uq = w_uq.reshape(w_uq.shape[0], MLA_HEADS, hd)
    uq = jnp.concatenate([uq[:, :, :MLA_NOPE].reshape(-1, MLA_HEADS * MLA_NOPE),
                          uq[:, :, MLA_NOPE:].reshape(-1, MLA_HEADS * MLA_ROPE)], axis=1)
    row = lambda v: v.reshape(1, -1)
    n_sub = peer_keys.shape[0] * peer_keys.shape[1]
    return dict(
        norm_mix=row(norm_mix), w_in=fused.astype(BF16), q_norm=row(mla_q_norm), w_uq=uq.astype(BF16),
        kv_norm=row(mla_kv_norm), w_ukt=jnp.transpose(w_uk, (1, 2, 0)).astype(BF16),
        w_uvh=jnp.transpose(w_uv, (1, 0, 2)).astype(BF16), w_o=w_o.astype(BF16),
        norm_mem=row(norm_mem), mem_norm=row(mem_norm), wq_x=wq_x.astype(BF16),
        w_kvx=jnp.concatenate([wk_x, wv_x], axis=1).astype(BF16), wo_x=wo_x.astype(BF16),
        norm_ffn=row(norm_ffn), w_pq=w_pq.astype(BF16),
        peer_keys=peer_keys.reshape(n_sub, peer_keys.shape[2], peer_keys.shape[3]).astype(BF16),
        peer_u=_pack_table(peer_u), peer_v=_pack_table(peer_v), norm_final=norm_final)


def _tile(n, want):
    t = min(n, want)
    while n % t:
        t //= 2
    return t


def _group_step(x, pos, b, wts, attend, mk, mv):
    n, d = x.shape
    t = n // b
    ckv, kr, kv, ik, ql, qr, dq, iq, iw = _mixer_in(x, pos, _tile(n, 512), wts)
    o_lat, o_dsa = attend(ql, qr, ckv, kr, dq, kv, iq, iw, ik)
    h = _mixer_out(o_lat, o_dsa, x, wts, _tile(n, 512))
    h = _cross(h, mk, mv, wts, b, _tile(t, 512))
    xn, e, gate = _peer_route(h, wts, _tile(n, 128), d // (2 * LANES))
    y = _peer_experts(xn, h, e, gate, wts, _tile(n, 64))
    return y, (ckv, kr, kv, ik)


def kernel(x_prompt, x_sample, cache_mla_ckv, cache_mla_kr, cache_dsa_kv, cache_dsa_idx, cache_mem_k, cache_mem_v, page_table, mem_prompt, norm_mix, w_in, mla_q_norm, w_uq, mla_kv_norm, w_uk, w_uv, w_o, norm_mem, mem_norm, wq_x, wk_x, wv_x, wo_x, norm_ffn, w_pq, peer_keys, peer_u, peer_v, norm_final):
    depth = w_in.shape[0]
    assert depth == 1, "one decoder layer"
    bp, tp, d = x_prompt.shape
    bs, ts, _ = x_sample.shape
    n_pool, page = cache_mla_ckv.shape[1], cache_mla_ckv.shape[2]
    assert page == PAGE
    past = page_table.shape[1] * page
    wts = _prep_weights(norm_mix[0], w_in[0], mla_q_norm[0], w_uq[0], mla_kv_norm[0], w_uk[0], w_uv[0], w_o[0],
                        norm_mem[0], mem_norm[0], wq_x[0], wk_x[0], wv_x[0], wo_x[0], norm_ffn[0], w_pq[0],
                        peer_keys[0], peer_u[0], peer_v[0], norm_final)
    k_prompt = min(DSA_TOPK, tp // 4)
    k_sample = min(DSA_TOPK, (past + ts) // 4)
    xw = X_HEADS * X_DIM
    n_mem = mem_prompt.shape[1]

    def attend_prompt(ql, qr, ckv, kr, dq, kv, iq, iw, ik):
        o_lat = _mla_prompt(ql, qr, ckv, kr, bp, tp, _tile(tp, 128), _tile(tp, 512))
        o_dsa = _dsa_prompt(iq, iw, ik, dq, kv, bp, tp, _tile(tp, 256), _tile(tp, 512), k_prompt)
        return o_lat, o_dsa

    g_pages = _tile(page_table.shape[1], 16)

    def attend_sample(ql, qr, ckv, kr, dq, kv, iq, iw, ik):
        kr_t = jnp.transpose(cache_mla_kr[0], (0, 2, 1))
        ik_t = jnp.transpose(cache_dsa_idx[0], (0, 2, 1))
        kv_t = jnp.transpose(cache_dsa_kv[0], (0, 2, 3, 4, 1)).reshape(n_pool, -1, page)
        o_lat = _mla_sample(page_table, ql, qr, ckv, kr, cache_mla_ckv[0], kr_t, bs, ts, g_pages)
        o_dsa = _dsa_sample(page_table, iq, iw, dq, ik, kv, ik_t, kv_t, bs, ts, g_pages, k_sample)
        return o_lat, o_dsa

    mk_p, mv_p = _mem_kv(mem_prompt.reshape(bp * n_mem, d), wts, _tile(bp * n_mem, 512))
    y_p, (ckv_p, kr_p, kv_p, ik_p) = _group_step(
        x_prompt.reshape(bp * tp, d), jnp.arange(tp), bp, wts, attend_prompt,
        mk_p.reshape(bp, n_mem, xw), mv_p.reshape(bp, n_mem, xw))
    y_s, (ckv_s, kr_s, kv_s, ik_s) = _group_step(
        x_sample.reshape(bs * ts, d), past + jnp.arange(ts), bs, wts, attend_sample,
        cache_mem_k[0].reshape(bs, n_mem, xw), cache_mem_v[0].reshape(bs, n_mem, xw))
    kv_shape = (2, DSA_KV_HEADS, DSA_DIM)
    return (y_p.reshape(bp, tp, d), y_s.reshape(bs, ts, d),
            ckv_p.reshape(1, bp, tp, -1), kr_p.reshape(1, bp, tp, -1), kv_p.reshape((1, bp, tp) + kv_shape),
            ik_p.reshape(1, bp, tp, -1), mk_p.reshape(1, bp, n_mem, X_HEADS, X_DIM),
            mv_p.reshape(1, bp, n_mem, X_HEADS, X_DIM),
            ckv_s.reshape(1, bs, ts, -1), kr_s.reshape(1, bs, ts, -1), kv_s.reshape((1, bs, ts) + kv_shape),
            ik_s.reshape(1, bs, ts, -1))
```

```python
import functools
import math

import numpy as np
import jax
import jax.numpy as jnp
from jax import lax
from jax.experimental import pallas as pl
from jax.experimental.pallas import tpu as pltpu

F32 = jnp.float32
BF16 = jnp.bfloat16
I32 = jnp.int32
U32 = jnp.uint32

NORM_EPS = 1e-6
ROPE_THETA = 10000.0
PAGE = 128
LANES = 128
SUBLANES = 8
VMEM_LIMIT = 56 * 1024 * 1024
NEG = -1e30
INT_MIN = -2 ** 31

MLA_HEADS, MLA_NOPE, MLA_ROPE, MLA_RANK = 8, 64, 32, 128
DSA_HEADS, DSA_KV_HEADS, DSA_DIM = 8, 2, 64
IDX_HEADS, IDX_DIM = 8, 64
DSA_TOPK = 256
X_HEADS, X_DIM = 4, 64
PEER_HEADS, PEER_KEYS, PEER_TOPK = 8, 128, 16

C_CQ, C_CKV, C_KR, C_DQ, C_DK, C_DV, C_IQ, C_IK, C_IW, C_END = 0, 256, 384, 512, 1024, 1152, 1280, 1792, 1920, 2048


def _cparams(sem):
    return pltpu.CompilerParams(dimension_semantics=sem, vmem_limit_bytes=VMEM_LIMIT)


def _rms(x, g):
    return x * lax.rsqrt(jnp.mean(x * x, axis=-1, keepdims=True) + NORM_EPS) * g


def _dot(a, b):
    return jnp.dot(a.astype(BF16), b.astype(BF16), preferred_element_type=F32)


def _dot_nt(a, b):
    return lax.dot_general(a.astype(BF16), b.astype(BF16), (((1,), (1,)), ((), ())), preferred_element_type=F32)


def _rope(x, cos, sin, d):
    w = x.shape[-1]
    half = d // 2
    lane = lax.broadcasted_iota(I32, x.shape, 1)
    first = (lane & (d - 1)) < half
    fwd = pltpu.roll(x, w - half, axis=1)
    bwd = pltpu.roll(x, half, axis=1)
    return x * cos + jnp.where(first, fwd, bwd) * sin


def _rope_tables(pos, d, rows):
    half = d // 2
    inv = ROPE_THETA ** (-jnp.arange(half, dtype=F32) / half)
    ang = pos.astype(F32)[:, None] * inv[None, :]
    cos, sin = jnp.cos(ang), jnp.sin(ang)
    cos = jnp.tile(jnp.concatenate([cos, cos], -1), (1, LANES // d))
    sin = jnp.tile(jnp.concatenate([-sin, sin], -1), (1, LANES // d))
    reps = max(1, rows // pos.shape[0])
    return jnp.tile(cos, (reps, 1)), jnp.tile(sin, (reps, 1))


def _mix_in_body(x_ref, g_ref, w_ref, qn_ref, wuq_ref, kvn_ref, wuk_ref, c64_ref, s64_ref, c32_ref, s32_ref,
                 ckv_ref, kr_ref, kv_ref, ik_ref, ql_ref, qr_ref, dq_ref, iq_ref, iw_ref):
    xn = _rms(x_ref[...], g_ref[...])
    p = _dot(xn, w_ref[...])
    c64, s64, c32, s32 = c64_ref[...], s64_ref[...], c32_ref[...], s32_ref[...]
    c64x4 = jnp.concatenate([c64] * 4, axis=1)
    s64x4 = jnp.concatenate([s64] * 4, axis=1)
    q = _dot(_rms(p[:, C_CQ:C_CKV], qn_ref[...]), wuq_ref[...])
    nope = MLA_HEADS * MLA_NOPE
    for h in range(MLA_HEADS):
        ql_ref[h] = _dot(q[:, h * MLA_NOPE:(h + 1) * MLA_NOPE], wuk_ref[h])
    qr_ref[...] = _rope(q[:, nope:], jnp.concatenate([c32] * 2, axis=1), jnp.concatenate([s32] * 2, axis=1), MLA_ROPE)
    ckv_ref[...] = _rms(p[:, C_CKV:C_KR], kvn_ref[...])
    kr_ref[...] = _rope(p[:, C_KR:C_DQ], c32, s32, MLA_ROPE)[:, :MLA_ROPE]
    dq_ref[...] = _rope(p[:, C_DQ:C_DK], c64x4, s64x4, DSA_DIM)
    kv_ref[...] = jnp.concatenate([_rope(p[:, C_DK:C_DV], c64, s64, DSA_DIM), p[:, C_DV:C_IQ]], axis=1)
    iq_ref[...] = _rope(p[:, C_IQ:C_IK], c64x4, s64x4, IDX_DIM)
    ik_ref[...] = _rope(p[:, C_IK:C_IW], c64, s64, IDX_DIM)[:, :IDX_DIM]
    iw_ref[...] = p[:, C_IW:C_IW + IDX_HEADS]


def _mixer_in(x, pos, tm, wts):
    n, d = x.shape
    t = pos.shape[0]
    c64, s64 = _rope_tables(pos, DSA_DIM, tm)
    c32, s32 = _rope_tables(pos, MLA_ROPE, tm)
    tb = c64.shape[0] // tm
    row = lambda i: (i, 0)
    tab = lambda i: (i % tb, 0)
    full2 = lambda i: (0, 0)
    full3 = lambda i: (0, 0, 0)
    out_shapes = [
        jax.ShapeDtypeStruct((n, MLA_RANK), F32), jax.ShapeDtypeStruct((n, MLA_ROPE), F32),
        jax.ShapeDtypeStruct((n, 2 * DSA_KV_HEADS * DSA_DIM), F32), jax.ShapeDtypeStruct((n, IDX_DIM), F32),
        jax.ShapeDtypeStruct((MLA_HEADS, n, MLA_RANK), F32), jax.ShapeDtypeStruct((n, MLA_HEADS * MLA_ROPE), F32),
        jax.ShapeDtypeStruct((n, DSA_HEADS * DSA_DIM), F32), jax.ShapeDtypeStruct((n, IDX_HEADS * IDX_DIM), F32),
        jax.ShapeDtypeStruct((n, IDX_HEADS), F32)]
    out_specs = [
        pl.BlockSpec((tm, MLA_RANK), row), pl.BlockSpec((tm, MLA_ROPE), row),
        pl.BlockSpec((tm, 2 * DSA_KV_HEADS * DSA_DIM), row), pl.BlockSpec((tm, IDX_DIM), row),
        pl.BlockSpec((MLA_HEADS, tm, MLA_RANK), lambda i: (0, i, 0)), pl.BlockSpec((tm, MLA_HEADS * MLA_ROPE), row),
        pl.BlockSpec((tm, DSA_HEADS * DSA_DIM), row), pl.BlockSpec((tm, IDX_HEADS * IDX_DIM), row),
        pl.BlockSpec((tm, IDX_HEADS), row)]
    in_specs = [
        pl.BlockSpec((tm, d), row), pl.BlockSpec((1, d), full2), pl.BlockSpec((d, C_END), full2),
        pl.BlockSpec((1, 256), full2), pl.BlockSpec((256, MLA_HEADS * (MLA_NOPE + MLA_ROPE)), full2),
        pl.BlockSpec((1, MLA_RANK), full2), pl.BlockSpec((MLA_HEADS, MLA_NOPE, MLA_RANK), full3),
        pl.BlockSpec((tm, LANES), tab), pl.BlockSpec((tm, LANES), tab),
        pl.BlockSpec((tm, LANES), tab), pl.BlockSpec((tm, LANES), tab)]
    return pl.pallas_call(
        _mix_in_body, grid=(n // tm,), in_specs=in_specs, out_specs=out_specs, out_shape=out_shapes,
        compiler_params=_cparams(("parallel",)), name="mixer_in",
    )(x, wts["norm_mix"], wts["w_in"], wts["q_norm"], wts["w_uq"], wts["kv_norm"], wts["w_ukt"], c64, s64, c32, s32)


def _online_update(m_ref, l_ref, acc_ref, s, mask, v):
    if mask is not None:
        s = jnp.where(mask, s, NEG)
    m_old = m_ref[...]
    m_new = jnp.maximum(m_old, jnp.max(s, axis=1, keepdims=True))
    p = jnp.exp(s - m_new)
    alpha = jnp.exp(m_old - m_new)
    l_ref[...] = alpha * l_ref[...] + jnp.sum(p, axis=1, keepdims=True)
    acc_ref[...] = alpha * acc_ref[...] + _dot(p, v)
    m_ref[...] = m_new


def _flash_step(m_ref, acc_ref, s, v1, v_is_transposed=False):
    m_old = m_ref[...]
    m_new = jnp.maximum(m_old, jnp.max(s, axis=1, keepdims=True))
    p = jnp.exp(s - m_new)
    acc_ref[...] = jnp.exp(m_old - m_new) * acc_ref[...] + (_dot_nt(p, v1) if v_is_transposed else _dot(p, v1))
    m_ref[...] = m_new


def _with_ones(v, axis=1):
    return jnp.concatenate([v, jnp.ones(v.shape, v.dtype)], axis=axis)


def _mask_heads(s, mask, group):
    r, k = s.shape
    return jnp.where(mask[None], s.reshape(group, r // group, k), NEG).reshape(r, k)


def _init_state(m_ref, l_ref, acc_ref):
    m_ref[...] = jnp.full(m_ref.shape, NEG, F32)
    l_ref[...] = jnp.zeros(l_ref.shape, F32)
    acc_ref[...] = jnp.zeros(acc_ref.shape, F32)


def _stack_heads(x, n_heads, width):
    return jnp.concatenate([x[:, h * width:(h + 1) * width] for h in range(n_heads)], axis=0)


def _mla_prompt_body(ql_ref, qr_ref, ckv_ref, kr_ref, o_ref, m_ref, l_ref, acc_ref, *, tq, tk, sub):
    i = pl.program_id(1)
    rows = MLA_HEADS * tq
    scale = (MLA_NOPE + MLA_ROPE) ** -0.5
    q_l = (ql_ref[...].reshape(rows, MLA_RANK) * scale).astype(BF16)
    q_r = (_stack_heads(qr_ref[...], MLA_HEADS, MLA_ROPE) * scale).astype(BF16)
    _init_state(m_ref, l_ref, acc_ref)

    def block(j, masked):
        k0 = pl.multiple_of(j * tk, tk)
        kc = ckv_ref[pl.ds(k0, tk), :].astype(BF16)
        kr = kr_ref[pl.ds(k0, tk), :].astype(BF16)
        for r0 in range(0, rows, sub):
            s = _dot_nt(q_l[r0:r0 + sub], kc) + _dot_nt(q_r[r0:r0 + sub], kr)
            mask = None
            if masked:
                row_t = i * tq + ((r0 + lax.broadcasted_iota(I32, (sub, 1), 0)) & (tq - 1))
                mask = (k0 + lax.broadcasted_iota(I32, (sub, tk), 1)) <= row_t
            sl = pl.ds(r0, sub)
            _online_update(m_ref.at[sl], l_ref.at[sl], acc_ref.at[sl], s, mask, kc)

    n_full = (i * tq + 1) // tk

    def step(j, carry):
        block(j, False)
        return carry

    lax.fori_loop(0, n_full, step, 0)
    block(n_full, True)
    o_ref[...] = (acc_ref[...] / l_ref[...]).reshape(MLA_HEADS, tq, MLA_RANK)


def _mla_prompt(ql, qr, ckv, kr, b, t, tq, tk):
    n = b * t
    nq = t // tq
    rows = MLA_HEADS * tq
    assert tk % tq == 0
    return pl.pallas_call(
        functools.partial(_mla_prompt_body, tq=tq, tk=tk, sub=min(rows, 256)),
        grid=(b, nq),
        in_specs=[pl.BlockSpec((MLA_HEADS, tq, MLA_RANK), lambda bi, i: (0, bi * nq + i, 0)),
                  pl.BlockSpec((tq, MLA_HEADS * MLA_ROPE), lambda bi, i: (bi * nq + i, 0)),
                  pl.BlockSpec((t, MLA_RANK), lambda bi, i: (bi, 0)),
                  pl.BlockSpec((t, MLA_ROPE), lambda bi, i: (bi, 0))],
        out_specs=pl.BlockSpec((MLA_HEADS, tq, MLA_RANK), lambda bi, i: (0, bi * nq + i, 0)),
        out_shape=jax.ShapeDtypeStruct((MLA_HEADS, n, MLA_RANK), F32),
        scratch_shapes=[pltpu.VMEM((rows, 1), F32), pltpu.VMEM((rows, 1), F32), pltpu.VMEM((rows, MLA_RANK), F32)],
        compiler_params=_cparams(("parallel", "arbitrary")), name="mla_prompt",
    )(ql, qr, ckv, kr)


def _order_key(score):
    bits = lax.bitcast_convert_type(score + 0.0, I32)
    return bits ^ ((bits >> 31) & 0x7FFFFFFF)


def _kth_largest(count_ge, shape, k):
    kf = float(k)
    tau0 = jnp.where(count_ge(jnp.zeros(shape, I32)) >= kf, 0, INT_MIN).astype(I32)

    def bit(it, tau):
        cand = tau | lax.shift_left(jnp.int32(1), 30 - it)
        return jnp.where(count_ge(cand) >= kf, cand, tau)

    return lax.fori_loop(0, 31, bit, tau0)


def _select_mask(key, valid, tau, need, carry, tri):
    masks = []
    for c in range(key.shape[1] // LANES):
        kb = key[:, c * LANES:(c + 1) * LANES]
        eq, gt = kb == tau, kb > tau
        if valid is not None:
            vb = valid[:, c * LANES:(c + 1) * LANES]
            eq, gt = eq & vb, gt & vb
        eqf = jnp.where(eq, 1.0, 0.0)
        before = carry + jnp.dot(eqf.astype(BF16), tri, preferred_element_type=F32)
        masks.append(gt | (eq & (before < need)))
        carry = carry + jnp.sum(eqf, axis=1, keepdims=True)
    return jnp.concatenate(masks, axis=1), carry


def _strict_upper(n):
    r = lax.broadcasted_iota(I32, (n, n), 0)
    c = lax.broadcasted_iota(I32, (n, n), 1)
    return jnp.where(r < c, 1.0, 0.0).astype(BF16)


def _index_scores(iq_s, iw_b, ikc, keys_transposed=False):
    t = iq_s.shape[0] // IDX_HEADS
    d = _dot(iq_s, ikc) if keys_transposed else _dot_nt(iq_s, ikc)
    score = None
    for h in range(IDX_HEADS):
        term = iw_b(h) * jnp.maximum(d[h * t:(h + 1) * t], 0.0)
        score = term if score is None else score + term
    return score


def _count_lanes(m):
    part = m[:, :LANES]
    for c in range(1, m.shape[1] // LANES):
        part = part + m[:, c * LANES:(c + 1) * LANES]
    return part


def _dsa_prompt_body(iq_ref, iw_ref, ik_ref, dq_ref, kv_ref, o_ref, key_ref, iwb_ref, m_ref, acc_ref, *, tq, ck, topk):
    i = pl.program_id(1)
    nc = ((i + 1) * tq + ck - 1) // ck
    group = DSA_HEADS // DSA_KV_HEADS
    iq_s = _stack_heads(iq_ref[...], IDX_HEADS, IDX_DIM).astype(BF16)
    iw = iw_ref[...]
    for h in range(IDX_HEADS):
        iwb_ref[h] = jnp.broadcast_to(iw[:, h:h + 1], (tq, LANES))
    iw_b = lambda h: jnp.concatenate([iwb_ref[h]] * (ck // LANES), axis=1)
    q_t = i * tq + lax.broadcasted_iota(I32, (tq, 1), 0)

    def causal(c):
        return (c * ck + lax.broadcasted_iota(I32, (tq, ck), 1)) <= q_t

    def score_chunk(c, carry):
        k0 = pl.multiple_of(c * ck, ck)
        score = _index_scores(iq_s, iw_b, ik_ref[pl.ds(k0, ck), :].astype(BF16))
        key_ref[c] = jnp.where(causal(c), _order_key(score), INT_MIN)
        return carry

    lax.fori_loop(0, nc, score_chunk, 0)

    def count(pred):
        def body(c, acc):
            return acc + _count_lanes(jnp.where(pred(key_ref[c]), 1.0, 0.0))
        return jnp.sum(lax.fori_loop(0, nc, body, jnp.zeros((tq, LANES), F32)), axis=1, keepdims=True)

    tau = _kth_largest(lambda cand: count(lambda k: k >= cand), (tq, 1), topk)
    need = float(topk) - count(lambda k: k > tau)
    tri = _strict_upper(LANES)
    dq = dq_ref[...] * DSA_DIM ** -0.5
    qs = [_stack_heads(dq[:, g * group * DSA_DIM:(g + 1) * group * DSA_DIM], group, DSA_DIM).astype(BF16)
          for g in range(DSA_KV_HEADS)]
    m_ref[...] = jnp.full(m_ref.shape, NEG, F32)
    acc_ref[...] = jnp.zeros(acc_ref.shape, F32)

    def attend(c, carry):
        k0 = pl.multiple_of(c * ck, ck)
        mask, carry = _select_mask(key_ref[c], causal(c), tau, need, carry, tri)
        kvc = kv_ref[pl.ds(k0, ck), :].astype(BF16)
        for g in range(DSA_KV_HEADS):
            kc = kvc[:, g * DSA_DIM:(g + 1) * DSA_DIM]
            v1 = _with_ones(kvc[:, (DSA_KV_HEADS + g) * DSA_DIM:(DSA_KV_HEADS + g + 1) * DSA_DIM])
            for n in range(group):
                sl = pl.ds(n * tq, tq)
                s = jnp.where(mask, _dot_nt(qs[g][n * tq:(n + 1) * tq], kc), NEG)
                _flash_step(m_ref.at[g, sl], acc_ref.at[g, sl], s, v1)
        return carry

    lax.fori_loop(0, nc, attend, jnp.zeros((tq, 1), F32))
    outs = []
    for g in range(DSA_KV_HEADS):
        acc = acc_ref[g]
        o = acc[:, :DSA_DIM] / acc[:, DSA_DIM:DSA_DIM + 1]
        outs += [o[n * tq:(n + 1) * tq] for n in range(group)]
    o_ref[...] = jnp.concatenate(outs, axis=1)


def _dsa_prompt(iq, iw, ik, dq, kv, b, t, tq, ck, topk):
    n = b * t
    nq = t // tq
    group = DSA_HEADS // DSA_KV_HEADS
    rows = group * tq
    qrow = lambda bi, i: (bi * nq + i, 0)
    brow = lambda bi, i: (bi, 0)
    return pl.pallas_call(
        functools.partial(_dsa_prompt_body, tq=tq, ck=ck, topk=topk),
        grid=(b, nq),
        in_specs=[pl.BlockSpec((tq, IDX_HEADS * IDX_DIM), qrow), pl.BlockSpec((tq, IDX_HEADS), qrow),
                  pl.BlockSpec((t, IDX_DIM), brow), pl.BlockSpec((tq, DSA_HEADS * DSA_DIM), qrow),
                  pl.BlockSpec((t, 2 * DSA_KV_HEADS * DSA_DIM), brow)],
        out_specs=pl.BlockSpec((tq, DSA_HEADS * DSA_DIM), qrow),
        out_shape=jax.ShapeDtypeStruct((n, DSA_HEADS * DSA_DIM), F32),
        scratch_shapes=[pltpu.VMEM((t // ck, tq, ck), I32), pltpu.VMEM((IDX_HEADS, tq, LANES), F32),
                        pltpu.VMEM((DSA_KV_HEADS, rows, 1), F32), pltpu.VMEM((DSA_KV_HEADS, rows, 2 * DSA_DIM), F32)],
        compiler_params=_cparams(("parallel", "arbitrary")), name="dsa_prompt",
    )(iq, iw, ik, dq, kv)


def _mla_sample_body(pt_ref, ql_ref, qr_ref, ckvn_ref, krn_ref, *rest, t, g_pages):
    ckv_pages = rest[:g_pages]
    kr_pages = rest[g_pages:2 * g_pages]
    o_ref, m_ref, l_ref, acc_ref = rest[2 * g_pages:]
    j = pl.program_id(1)
    rows = MLA_HEADS * t
    scale = (MLA_NOPE + MLA_ROPE) ** -0.5
    q_l = (ql_ref[...].reshape(rows, MLA_RANK) * scale).astype(BF16)
    q_r = (_stack_heads(qr_ref[...], MLA_HEADS, MLA_ROPE) * scale).astype(BF16)

    @pl.when(j == 0)
    def _():
        _init_state(m_ref, l_ref, acc_ref)

    kc = jnp.concatenate([r[0] for r in ckv_pages], axis=0).astype(BF16)
    kr_t = jnp.concatenate([r[0] for r in kr_pages], axis=1).astype(BF16)
    _online_update(m_ref, l_ref, acc_ref, _dot_nt(q_l, kc) + _dot(q_r, kr_t), None, kc)

    @pl.when(j == pl.num_programs(1) - 1)
    def _():
        kn = ckvn_ref[0].astype(BF16)
        s = _dot_nt(q_l, kn) + _dot_nt(q_r, krn_ref[0].astype(BF16))
        row_t = lax.broadcasted_iota(I32, (rows, PAGE), 0) & (t - 1)
        _online_update(m_ref, l_ref, acc_ref, s, lax.broadcasted_iota(I32, (rows, PAGE), 1) <= row_t, kn)
        o_ref[...] = (acc_ref[...] / l_ref[...]).reshape(MLA_HEADS, t, MLA_RANK)


def _pad_new(x, b, t):
    return jnp.pad(x.reshape(b, t, x.shape[-1]), ((0, 0), (0, PAGE - t), (0, 0)))


def _mla_sample(page_table, ql, qr, ckv_new, kr_new, pool_ckv, pool_kr_t, b, t, g_pages):
    n_pages = page_table.shape[1]
    steps = n_pages // g_pages
    rows = MLA_HEADS * t
    page = lambda g: (lambda bi, j, pt: (pt[bi, j * g_pages + g], 0, 0))
    seq3 = lambda bi, j, pt: (bi, 0, 0)
    in_specs = [pl.BlockSpec((MLA_HEADS, t, MLA_RANK), lambda bi, j, pt: (0, bi, 0)),
                pl.BlockSpec((t, MLA_HEADS * MLA_ROPE), lambda bi, j, pt: (bi, 0)),
                pl.BlockSpec((1, PAGE, MLA_RANK), seq3), pl.BlockSpec((1, PAGE, MLA_ROPE), seq3)]
    in_specs += [pl.BlockSpec((1, PAGE, MLA_RANK), page(g)) for g in range(g_pages)]
    in_specs += [pl.BlockSpec((1, MLA_ROPE, PAGE), page(g)) for g in range(g_pages)]
    grid_spec = pltpu.PrefetchScalarGridSpec(
        num_scalar_prefetch=1, grid=(b, steps), in_specs=in_specs,
        out_specs=pl.BlockSpec((MLA_HEADS, t, MLA_RANK), lambda bi, j, pt: (0, bi, 0)),
        scratch_shapes=[pltpu.VMEM((rows, 1), F32), pltpu.VMEM((rows, 1), F32), pltpu.VMEM((rows, MLA_RANK), F32)])
    return pl.pallas_call(
        functools.partial(_mla_sample_body, t=t, g_pages=g_pages), grid_spec=grid_spec,
        out_shape=jax.ShapeDtypeStruct((MLA_HEADS, b * t, MLA_RANK), F32),
        compiler_params=_cparams(("parallel", "arbitrary")), name="mla_sample",
    )(page_table, ql, qr, _pad_new(ckv_new, b, t), _pad_new(kr_new, b, t),
      *([pool_ckv] * g_pages), *([pool_kr_t] * g_pages))


def _dsa_sample_body(pt_ref, iq_ref, iw_ref, dq_ref, ikn_ref, kvn_ref, *rest, t, g_pages, steps, topk):
    ik_pages = rest[:g_pages]
    kv_pages = rest[g_pages:2 * g_pages]
    o_ref, key_ref, keyn_ref, tau_ref, need_ref, carry_ref, m_ref, acc_ref = rest[2 * g_pages:]
    j = pl.program_id(1)
    ck = g_pages * PAGE
    group = DSA_HEADS // DSA_KV_HEADS
    scale = DSA_DIM ** -0.5
    new_valid = lax.broadcasted_iota(I32, (t, PAGE), 1) <= lax.broadcasted_iota(I32, (t, PAGE), 0)

    @pl.when(j < steps)
    def _():
        iq_s = _stack_heads(iq_ref[...], IDX_HEADS, IDX_DIM).astype(BF16)
        iw = iw_ref[...]
        ik_t = jnp.concatenate([r[0] for r in ik_pages], axis=1).astype(BF16)
        key_ref[j] = _order_key(_index_scores(iq_s, lambda h: jnp.broadcast_to(iw[:, h:h + 1], (t, ck)), ik_t, True))

        @pl.when(j == steps - 1)
        def _():
            sn = _index_scores(iq_s, lambda h: jnp.broadcast_to(iw[:, h:h + 1], (t, PAGE)), ikn_ref[0].astype(BF16))
            keyn_ref[...] = jnp.where(new_valid, _order_key(sn), INT_MIN)

    @pl.when(j == steps)
    def _():
        def count(pred):
            acc = jnp.where(pred(keyn_ref[...]), 1.0, 0.0)
            for c in range(steps):
                acc = acc + _count_lanes(jnp.where(pred(key_ref[c]), 1.0, 0.0))
            return jnp.sum(acc, axis=1, keepdims=True)

        tau = _kth_largest(lambda cand: count(lambda k: k >= cand), (t, 1), topk)
        tau_ref[...] = tau
        need_ref[...] = float(topk) - count(lambda k: k > tau)
        carry_ref[...] = jnp.zeros((t, 1), F32)
        m_ref[...] = jnp.full(m_ref.shape, NEG, F32)
        acc_ref[...] = jnp.zeros(acc_ref.shape, F32)

    def attend(key, valid, kvc, transposed):
        mask, carry = _select_mask(key, valid, tau_ref[...], need_ref[...], carry_ref[...], _strict_upper(LANES))
        carry_ref[...] = carry
        dq = dq_ref[...] * scale
        for g in range(DSA_KV_HEADS):
            qs = _stack_heads(dq[:, g * group * DSA_DIM:(g + 1) * group * DSA_DIM], group, DSA_DIM)
            ks = slice(g * DSA_DIM, (g + 1) * DSA_DIM)
            vs = slice((DSA_KV_HEADS + g) * DSA_DIM, (DSA_KV_HEADS + g + 1) * DSA_DIM)
            if transposed:
                s, v1 = _dot(qs, kvc[ks]), _with_ones(kvc[vs], axis=0)
            else:
                s, v1 = _dot_nt(qs, kvc[:, ks]), _with_ones(kvc[:, vs])
            _flash_step(m_ref.at[g], acc_ref.at[g], _mask_heads(s, mask, group), v1, transposed)

    @pl.when(j >= steps)
    def _():
        kv_t = jnp.concatenate([r[0] for r in kv_pages], axis=1).astype(BF16)
        attend(key_ref[j - steps], None, kv_t, True)

        @pl.when(j == 2 * steps - 1)
        def _():
            attend(keyn_ref[...], new_valid, kvn_ref[0].astype(BF16), False)
            outs = []
            for g in range(DSA_KV_HEADS):
                acc = acc_ref[g]
                o = acc[:, :DSA_DIM] / acc[:, DSA_DIM:DSA_DIM + 1]
                outs += [o[n * t:(n + 1) * t] for n in range(group)]
            o_ref[...] = jnp.concatenate(outs, axis=1)


def _dsa_sample(page_table, iq, iw, dq, ik_new, kv_new, pool_ik_t, pool_kv_t, b, t, g_pages, topk):
    n_pages = page_table.shape[1]
    steps = n_pages // g_pages
    group = DSA_HEADS // DSA_KV_HEADS
    rows = group * t
    kvw = 2 * DSA_KV_HEADS * DSA_DIM
    ik_page = lambda g: (lambda bi, j, pt: (pt[bi, jnp.minimum(j, steps - 1) * g_pages + g], 0, 0))
    kv_page = lambda g: (lambda bi, j, pt: (pt[bi, jnp.maximum(j - steps, 0) * g_pages + g], 0, 0))
    seq2 = lambda bi, j, pt: (bi, 0)
    seq3 = lambda bi, j, pt: (bi, 0, 0)
    in_specs = [pl.BlockSpec((t, IDX_HEADS * IDX_DIM), seq2), pl.BlockSpec((t, IDX_HEADS), seq2),
                pl.BlockSpec((t, DSA_HEADS * DSA_DIM), seq2),
                pl.BlockSpec((1, PAGE, IDX_DIM), seq3), pl.BlockSpec((1, PAGE, kvw), seq3)]
    in_specs += [pl.BlockSpec((1, IDX_DIM, PAGE), ik_page(g)) for g in range(g_pages)]
    in_specs += [pl.BlockSpec((1, kvw, PAGE), kv_page(g)) for g in range(g_pages)]
    grid_spec = pltpu.PrefetchScalarGridSpec(
        num_scalar_prefetch=1, grid=(b, 2 * steps), in_specs=in_specs,
        out_specs=pl.BlockSpec((t, DSA_HEADS * DSA_DIM), seq2),
        scratch_shapes=[pltpu.VMEM((steps, t, g_pages * PAGE), I32), pltpu.VMEM((t, PAGE), I32),
                        pltpu.VMEM((t, 1), I32), pltpu.VMEM((t, 1), F32), pltpu.VMEM((t, 1), F32),
                        pltpu.VMEM((DSA_KV_HEADS, rows, 1), F32), pltpu.VMEM((DSA_KV_HEADS, rows, 2 * DSA_DIM), F32)])
    return pl.pallas_call(
        functools.partial(_dsa_sample_body, t=t, g_pages=g_pages, steps=steps, topk=topk), grid_spec=grid_spec,
        out_shape=jax.ShapeDtypeStruct((b * t, DSA_HEADS * DSA_DIM), F32),
        compiler_params=_cparams(("parallel", "arbitrary")), name="dsa_sample",
    )(page_table, iq, iw, dq, _pad_new(ik_new, b, t), _pad_new(kv_new, b, t),
      *([pool_ik_t] * g_pages), *([pool_kv_t] * g_pages))


def _mix_out_body(ol_ref, od_ref, h_ref, wuv_ref, wo_ref, o_ref):
    mla = jnp.concatenate([_dot(ol_ref[h], wuv_ref[h]) for h in range(MLA_HEADS)], axis=1)
    half = mla.shape[1]
    o_ref[...] = h_ref[...] + _dot(mla, wo_ref[:half, :]) + _dot(od_ref[...], wo_ref[half:, :])


def _mixer_out(o_lat, o_dsa, h, wts, tm):
    n, d = h.shape
    row = lambda i: (i, 0)
    v_dim = wts["w_uvh"].shape[-1]
    mix = wts["w_o"].shape[0]
    return pl.pallas_call(
        _mix_out_body, grid=(n // tm,),
        in_specs=[pl.BlockSpec((MLA_HEADS, tm, MLA_RANK), lambda i: (0, i, 0)), pl.BlockSpec((tm, DSA_HEADS * DSA_DIM), row),
                  pl.BlockSpec((tm, d), row), pl.BlockSpec((MLA_HEADS, MLA_RANK, v_dim), lambda i: (0, 0, 0)),
                  pl.BlockSpec((mix, d), lambda i: (0, 0))],
        out_specs=pl.BlockSpec((tm, d), row), out_shape=jax.ShapeDtypeStruct((n, d), F32),
        compiler_params=_cparams(("parallel",)), name="mixer_out",
    )(o_lat, o_dsa, h, wts["w_uvh"], wts["w_o"])


def _mem_kv_body(m_ref, g_ref, w_ref, k_ref, v_ref):
    p = _dot(_rms(m_ref[...], g_ref[...]), w_ref[...])
    half = p.shape[1] // 2
    k_ref[...] = p[:, :half]
    v_ref[...] = p[:, half:]


def _mem_kv(mem, wts, tm):
    n, d = mem.shape
    w = wts["w_kvx"]
    half = w.shape[1] // 2
    row = lambda i: (i, 0)
    return pl.pallas_call(
        _mem_kv_body, grid=(n // tm,),
        in_specs=[pl.BlockSpec((tm, d), row), pl.BlockSpec((1, d), lambda i: (0, 0)), pl.BlockSpec(w.shape, lambda i: (0, 0))],
        out_specs=[pl.BlockSpec((tm, half), row)] * 2, out_shape=[jax.ShapeDtypeStruct((n, half), F32)] * 2,
        compiler_params=_cparams(("parallel",)), name="mem_kv",
    )(mem, wts["mem_norm"], w)


def _cross_body(h_ref, g_ref, wq_ref, k_ref, v_ref, wo_ref, o_ref):
    h = h_ref[...]
    q = _dot(_rms(h, g_ref[...]), wq_ref[...])
    k = k_ref[0].astype(BF16)
    v = v_ref[0].astype(BF16)
    outs = []
    for hd in range(X_HEADS):
        sl = slice(hd * X_DIM, (hd + 1) * X_DIM)
        s = _dot_nt(q[:, sl], k[:, sl]) * X_DIM ** -0.5
        e = jnp.exp(s - jnp.max(s, axis=1, keepdims=True))
        p = e / jnp.sum(e, axis=1, keepdims=True)
        outs.append(_dot(p, v[:, sl]))
    o_ref[...] = h + _dot(jnp.concatenate(outs, axis=1), wo_ref[...])


def _cross(h, mk, mv, wts, b, tm):
    n, d = h.shape
    per = (n // b) // tm
    m, w = mk.shape[1], mk.shape[2]
    row = lambda i: (i, 0)
    mem = lambda i: (i // per, 0, 0)
    return pl.pallas_call(
        _cross_body, grid=(n // tm,),
        in_specs=[pl.BlockSpec((tm, d), row), pl.BlockSpec((1, d), lambda i: (0, 0)), pl.BlockSpec((d, w), lambda i: (0, 0)),
                  pl.BlockSpec((1, m, w), mem), pl.BlockSpec((1, m, w), mem), pl.BlockSpec((w, d), lambda i: (0, 0))],
        out_specs=pl.BlockSpec((tm, d), row), out_shape=jax.ShapeDtypeStruct((n, d), F32),
        compiler_params=_cparams(("parallel",)), name="cross_attn",
    )(h, wts["norm_mem"], wts["wq_x"], mk, mv, wts["wo_x"])


def _extract_topk(s, payload, k):
    r = s.shape[0]
    row = lax.broadcasted_iota(I32, s.shape, 0)
    vals, pays = [], []
    for _ in range(k):
        m = jnp.max(s, axis=0, keepdims=True)
        first = jnp.min(jnp.where(s == m, row, r), axis=0, keepdims=True)
        hit = row == first
        vals.append(m)
        pays.append(first if payload is None else jnp.sum(jnp.where(hit, payload, 0), axis=0, keepdims=True))
        s = jnp.where(hit, -jnp.inf, s)
    return jnp.concatenate(vals, axis=0), jnp.concatenate(pays, axis=0)


def _pair_blocks(k):
    blocks = []
    a = 0
    while a < k and k // (a + 1) > 1:
        nb = k // (a + 1)
        blocks.append((a, nb, -(-nb // SUBLANES) * SUBLANES))
        a += 1
    return blocks, a


ROUTE_HEADS_PER_ITER = 2


def _peer_route_body(h_ref, g_ref, wpq_ref, keys_ref, xn_ref, e_ref, gate_ref, q_scr, gs_scr, ge_scr, *, row_words):
    tm = h_ref.shape[0]
    k = PEER_TOPK
    xn = _rms(h_ref[...], g_ref[...])
    xn_ref[...] = xn
    q = _dot(xn, wpq_ref[...])
    n_sub = 2 * PEER_HEADS
    kd = q.shape[1] // n_sub
    for hp in range(n_sub):
        q_scr[hp] = q[:, hp * kd:(hp + 1) * kd].astype(BF16)
    blocks, a_tail = _pair_blocks(k)

    def route_head(h):
        (t1, i1), (t2, i2) = [_extract_topk(_dot_nt(keys_ref[2 * h + p], q_scr[2 * h + p]), None, k) for p in range(2)]
        cand, ids = [], []
        for a, nb, rows in blocks:
            live = lax.broadcasted_iota(I32, (rows, tm), 0) < nb
            cand.append(jnp.where(live, t1[a:a + 1] + t2[:rows], -jnp.inf))
            ids.append(i1[a:a + 1] * PEER_KEYS + i2[:rows])
        cand.append(t1[a_tail:] + t2[0:1])
        ids.append(i1[a_tail:] * PEER_KEYS + i2[0:1])
        gs_scr[h], ge_scr[h] = _extract_topk(jnp.concatenate(cand, axis=0), jnp.concatenate(ids, axis=0), k)

    def heads(it, carry):
        for j in range(ROUTE_HEADS_PER_ITER):
            route_head(it * ROUTE_HEADS_PER_ITER + j)
        return carry

    lax.fori_loop(0, PEER_HEADS // ROUTE_HEADS_PER_ITER, heads, 0)
    gs = gs_scr[...]
    e = jnp.exp(gs - jnp.max(gs, axis=1, keepdims=True))
    gate = e / jnp.sum(e, axis=1, keepdims=True)
    gate_ref[...] = gate.reshape(PEER_HEADS * k, tm).T
    e_ref[...] = (ge_scr[...] * row_words).reshape(PEER_HEADS * k, tm).T


def _peer_route(h, wts, tm, row_words):
    n, d = h.shape
    slots = PEER_HEADS * PEER_TOPK
    keys = wts["peer_keys"]
    n_sub, n_keys, kd = keys.shape
    row = lambda i: (i, 0)
    return pl.pallas_call(
        functools.partial(_peer_route_body, row_words=row_words), grid=(n // tm,),
        in_specs=[pl.BlockSpec((tm, d), row), pl.BlockSpec((1, d), lambda i: (0, 0)),
                  pl.BlockSpec((d, n_sub * kd), lambda i: (0, 0)), pl.BlockSpec(keys.shape, lambda i: (0, 0, 0))],
        out_specs=[pl.BlockSpec((tm, d), row), pl.BlockSpec((tm, slots), row), pl.BlockSpec((tm, slots), row)],
        out_shape=[jax.ShapeDtypeStruct((n, d), F32), jax.ShapeDtypeStruct((n, slots), I32),
                   jax.ShapeDtypeStruct((n, slots), F32)],
        scratch_shapes=[pltpu.VMEM((n_sub, tm, kd), BF16), pltpu.VMEM((PEER_HEADS, PEER_TOPK, tm), F32),
                        pltpu.VMEM((PEER_HEADS, PEER_TOPK, tm), I32)],
        compiler_params=_cparams(("parallel",)), name="peer_route",
    )(h, wts["norm_ffn"], wts["w_pq"], keys)


def _pack_table(w):
    e, d = w.shape
    bits = lax.bitcast_convert_type(w.astype(jnp.bfloat16), jnp.uint16).astype(U32)
    packed = bits[:, :d // 2] | (bits[:, d // 2:] << 16)
    return packed.reshape(e * d // (2 * LANES), LANES)


def _load_table(tbl_hbm, tbl_ref, sem):
    @pl.when(pl.program_id(0) == 0)
    def _():
        cp = pltpu.make_async_copy(tbl_hbm, tbl_ref, sem)
        cp.start()
        cp.wait()


GATHER_GROUP = 4


def _place_mask(rows, slots):
    vr = 2 * rows
    col = lax.broadcasted_iota(I32, (vr, vr * slots), 1) & (vr - 1)
    row = lax.broadcasted_iota(I32, (vr, vr * slots), 0)
    return col == 2 * (row & (rows - 1)) + row // rows


def _gather_pipeline(e_ref, tbl_ref, stage_ref, consume, tb, slots, rows):
    grp = GATHER_GROUP

    def stage(buf, n):
        n = jnp.minimum(n, tb - 1)
        for k in range(slots):
            e_grp = e_ref.at[n, pl.ds((k // 8) * 8, 8)]
            stage_ref[buf, pl.ds(k * rows, rows), :] = tbl_ref[pl.ds(pl.multiple_of(e_grp[k % 8], rows), rows), :]

    for j in range(grp):
        stage(j, j)

    def groups(it, carry):
        n0 = it * 2 * grp
        for half in range(2):
            for j in range(grp):
                consume(half * grp + j, n0 + half * grp + j)
            for j in range(grp):
                stage((1 - half) * grp + j, n0 + (half + 1) * grp + j)
        return carry

    lax.fori_loop(0, tb // (2 * grp), groups, 0)


def _peer_up_body(e_ref, x_ref, gate_ref, gather_ref, tbl_hbm, a_ref, tbl_ref, stage_ref, part_ref, sem,
                  *, tb, slots, rows):
    _load_table(tbl_hbm, tbl_ref, sem)
    vr = 2 * rows
    place = _place_mask(rows, slots)

    def consume(buf, n):
        b = pltpu.bitcast(stage_ref[buf], jnp.bfloat16)
        xv = x_ref[pl.ds(pl.multiple_of(n * vr, vr), vr), :]
        prod = jnp.where(place, _dot_nt(xv, b), 0.0)
        part_ref[n] = jnp.concatenate(
            [jnp.sum(prod[:, j * LANES:(j + 1) * LANES], axis=0, keepdims=True) for j in range(vr)], axis=0)

    _gather_pipeline(e_ref, tbl_ref, stage_ref, consume, tb, slots, rows)
    part = part_ref[...].reshape(tb * vr, LANES)
    p1 = part.astype(BF16)
    r1 = part - p1.astype(F32)
    p2 = r1.astype(BF16)
    p3 = (r1 - p2.astype(F32)).astype(BF16)
    g = gather_ref[...]
    hp = (jnp.dot(p1, g, preferred_element_type=F32) + jnp.dot(p2, g, preferred_element_type=F32)
          + jnp.dot(p3, g, preferred_element_type=F32))
    per = LANES // vr
    own = (lax.broadcasted_iota(I32, (tb * vr, slots), 1) // per) == (lax.broadcasted_iota(I32, (tb * vr, slots), 0) & (vr - 1))
    h = jnp.sum(jnp.where(own, hp, 0.0).reshape(tb, vr, slots), axis=1)
    a_ref[...] = gate_ref[...] * (0.5 * h * (1.0 + lax.erf(h * (2.0 ** -0.5))))


def _peer_down_body(e_ref, a_ref, h_ref, g_ref, spread_ref, tbl_hbm, y_ref, tbl_ref, stage_ref, arep_ref, sem,
                    *, tb, slots, rows):
    _load_table(tbl_hbm, tbl_ref, sem)
    d = 2 * rows * LANES
    vr = 2 * rows
    wide = vr * slots
    arep_ref[...] = _dot(a_ref[...], spread_ref[...])
    place = _place_mask(rows, slots)
    sub = lax.broadcasted_iota(I32, (vr, wide), 0)

    def consume(buf, n):
        b = pltpu.bitcast(stage_ref[buf], jnp.bfloat16)
        n8 = pl.multiple_of((n >> 3) << 3, 8)
        a8 = arep_ref[pl.ds(n8, 8), :]
        a_n = jnp.sum(jnp.where(sub == (n & 7), a8, 0.0), axis=0, keepdims=True)
        lhs = jnp.where(place, jnp.broadcast_to(a_n, (vr, wide)), 0.0)
        r0 = pl.multiple_of(n * vr, vr)
        y_ref[pl.ds(r0, vr), :] = h_ref[pl.ds(r0, vr), :] + _dot(lhs, b)

    _gather_pipeline(e_ref, tbl_ref, stage_ref, consume, tb, slots, rows)
    h = y_ref[...].reshape(tb, 2 * rows, LANES)
    ss = jnp.sum(jnp.sum(h * h, axis=2, keepdims=True), axis=1, keepdims=True)
    y_ref[...] = (h * lax.rsqrt(ss / d + NORM_EPS) * g_ref[...][None]).reshape(tb * 2 * rows, LANES)


def _peer_experts(xn, h, e, gate, wts, tb):
    n, d = h.shape
    slots = e.shape[1]
    assert slots == LANES
    rows = d // (2 * LANES)
    vr = 2 * rows
    tok = lambda i: (i, 0)
    smem = functools.partial(pl.BlockSpec, memory_space=pltpu.SMEM)
    tbl_shape = wts["peer_u"].shape
    common = dict(grid=(n // tb,), compiler_params=_cparams(("arbitrary",)))
    const = lambda i: (0, 0)
    stage = pltpu.VMEM((2 * GATHER_GROUP, slots * rows, LANES), U32)
    spread = (jnp.arange(vr * slots)[None, :] // vr == jnp.arange(slots)[:, None]).astype(BF16)
    lane_sum = (jnp.arange(LANES)[:, None] // vr == jnp.arange(slots)[None, :] % (LANES // vr)).astype(BF16)
    a = pl.pallas_call(
        functools.partial(_peer_up_body, tb=tb, slots=slots, rows=rows),
        in_specs=[smem((tb, slots), tok), pl.BlockSpec((tb * vr, LANES), tok), pl.BlockSpec((tb, slots), tok),
                  pl.BlockSpec((LANES, slots), const), pl.BlockSpec(memory_space=pl.ANY)],
        out_specs=pl.BlockSpec((tb, slots), tok), out_shape=jax.ShapeDtypeStruct((n, slots), F32),
        scratch_shapes=[pltpu.VMEM(tbl_shape, U32), stage, pltpu.VMEM((tb, vr, LANES), F32),
                        pltpu.SemaphoreType.DMA(())],
        name="peer_up", **common,
    )(e, xn.reshape(n * vr, LANES), gate, lane_sum, wts["peer_u"])
    y = pl.pallas_call(
        functools.partial(_peer_down_body, tb=tb, slots=slots, rows=rows),
        in_specs=[smem((tb, slots), tok), pl.BlockSpec((tb, slots), tok), pl.BlockSpec((tb * vr, LANES), tok),
                  pl.BlockSpec((vr, LANES), const), pl.BlockSpec((slots, vr * slots), const),
                  pl.BlockSpec(memory_space=pl.ANY)],
        out_specs=pl.BlockSpec((tb * vr, LANES), tok), out_shape=jax.ShapeDtypeStruct((n * vr, LANES), F32),
        scratch_shapes=[pltpu.VMEM(tbl_shape, U32), stage, pltpu.VMEM((tb, vr * slots), F32),
                        pltpu.SemaphoreType.DMA(())],
        name="peer_down", **common,
    )(e, a, h.reshape(n * vr, LANES), wts["norm_final"].reshape(vr, LANES), spread, wts["peer_v"])
    return y.reshape(n, d)


def _prep_weights(norm_mix, w_in, mla_q_norm, w_uq, mla_kv_norm, w_uk, w_uv, w_o, norm_mem, mem_norm, wq_x, wk_x,
                  wv_x, wo_x, norm_ffn, w_pq, peer_keys, peer_u, peer_v, norm_final):
    d = w_in.shape[0]
    widths = (256, 128, 32, 512, 128, 128, 512, 8, 64)
    offs = np.concatenate([[0], np.cumsum(widths)])
    cq, ckv, kr, dq, dk, dv, iq, iw, ik = [w_in[:, offs[i]:offs[i + 1]] for i in range(9)]
    pad = lambda w, to: jnp.pad(w, ((0, 0), (0, to - w.shape[1])))
    fused = jnp.concatenate([cq, ckv, pad(kr, 128), dq, dk, dv, iq, pad(ik, 128), pad(iw, 128)], axis=1)
    hd = MLA_NOPE + MLA_ROPE
    uq = w_uq.reshape(w_uq.shape[0], MLA_HEADS, hd)
    uq = jnp.concatenate([uq[:, :, :MLA_NOPE].reshape(-1, MLA_HEADS * MLA_NOPE),
                          uq[:, :, MLA_NOPE:].reshape(-1, MLA_HEADS * MLA_ROPE)], axis=1)
    row = lambda v: v.reshape(1, -1)
    n_sub = peer_keys.shape[0] * peer_keys.shape[1]
    return dict(
        norm_mix=row(norm_mix), w_in=fused.astype(BF16), q_norm=row(mla_q_norm), w_uq=uq.astype(BF16),
        kv_norm=row(mla_kv_norm), w_ukt=jnp.transpose(w_uk, (1, 2, 0)).astype(BF16),
        w_uvh=jnp.transpose(w_uv, (1, 0, 2)).astype(BF16), w_o=w_o.astype(BF16),
        norm_mem=row(norm_mem), mem_norm=row(mem_norm), wq_x=wq_x.astype(BF16),
        w_kvx=jnp.concatenate([wk_x, wv_x], axis=1).astype(BF16), wo_x=wo_x.astype(BF16),
        norm_ffn=row(norm_ffn), w_pq=w_pq.astype(BF16),
        peer_keys=peer_keys.reshape(n_sub, peer_keys.shape[2], peer_keys.shape[3]).astype(BF16),
        peer_u=_pack_table(peer_u), peer_v=_pack_table(peer_v), norm_final=norm_final)


def _tile(n, want):
    t = min(n, want)
    while n % t:
        t //= 2
    return t


def _group_step(x, pos, b, wts, attend, mk, mv):
    n, d = x.shape
    t = n // b
    ckv, kr, kv, ik, ql, qr, dq, iq, iw = _mixer_in(x, pos, _tile(n, 512), wts)
    o_lat, o_dsa = attend(ql, qr, ckv, kr, dq, kv, iq, iw, ik)
    h = _mixer_out(o_lat, o_dsa, x, wts, _tile(n, 512))
    h = _cross(h, mk, mv, wts, b, _tile(t, 512))
    xn, e, gate = _peer_route(h, wts, _tile(n, 128), d // (2 * LANES))
    y = _peer_experts(xn, h, e, gate, wts, _tile(n, 64))
    return y, (ckv, kr, kv, ik)


def kernel(x_prompt, x_sample, cache_mla_ckv, cache_mla_kr, cache_dsa_kv, cache_dsa_idx, cache_mem_k, cache_mem_v, page_table, mem_prompt, norm_mix, w_in, mla_q_norm, w_uq, mla_kv_norm, w_uk, w_uv, w_o, norm_mem, mem_norm, wq_x, wk_x, wv_x, wo_x, norm_ffn, w_pq, peer_keys, peer_u, peer_v, norm_final):
    depth = w_in.shape[0]
    assert depth == 1, "one decoder layer"
    bp, tp, d = x_prompt.shape
    bs, ts, _ = x_sample.shape
    n_pool, page = cache_mla_ckv.shape[1], cache_mla_ckv.shape[2]
    assert page == PAGE
    past = page_table.shape[1] * page
    wts = _prep_weights(norm_mix[0], w_in[0], mla_q_norm[0], w_uq[0], mla_kv_norm[0], w_uk[0], w_uv[0], w_o[0],
                        norm_mem[0], mem_norm[0], wq_x[0], wk_x[0], wv_x[0], wo_x[0], norm_ffn[0], w_pq[0],
                        peer_keys[0], peer_u[0], peer_v[0], norm_final)
    k_prompt = min(DSA_TOPK, tp // 4)
    k_sample = min(DSA_TOPK, (past + ts) // 4)
    xw = X_HEADS * X_DIM
    n_mem = mem_prompt.shape[1]

    def attend_prompt(ql, qr, ckv, kr, dq, kv, iq, iw, ik):
        o_lat = _mla_prompt(ql, qr, ckv, kr, bp, tp, _tile(tp, 128), _tile(tp, 512))
        o_dsa = _dsa_prompt(iq, iw, ik, dq, kv, bp, tp, _tile(tp, 256), _tile(tp, 512), k_prompt)
        return o_lat, o_dsa

    g_pages = _tile(page_table.shape[1], 16)

    def attend_sample(ql, qr, ckv, kr, dq, kv, iq, iw, ik):
        kr_t = jnp.transpose(cache_mla_kr[0], (0, 2, 1))
        ik_t = jnp.transpose(cache_dsa_idx[0], (0, 2, 1))
        kv_t = jnp.transpose(cache_dsa_kv[0], (0, 2, 3, 4, 1)).reshape(n_pool, -1, page)
        o_lat = _mla_sample(page_table, ql, qr, ckv, kr, cache_mla_ckv[0], kr_t, bs, ts, g_pages)
        o_dsa = _dsa_sample(page_table, iq, iw, dq, ik, kv, ik_t, kv_t, bs, ts, g_pages, k_sample)
        return o_lat, o_dsa

    mk_p, mv_p = _mem_kv(mem_prompt.reshape(bp * n_mem, d), wts, _tile(bp * n_mem, 512))
    y_p, (ckv_p, kr_p, kv_p, ik_p) = _group_step(
        x_prompt.reshape(bp * tp, d), jnp.arange(tp), bp, wts, attend_prompt,
        mk_p.reshape(bp, n_mem, xw), mv_p.reshape(bp, n_mem, xw))
    y_s, (ckv_s, kr_s, kv_s, ik_s) = _group_step(
        x_sample.reshape(bs * ts, d), past + jnp.arange(ts), bs, wts, attend_sample,
        cache_mem_k[0].reshape(bs, n_mem, xw), cache_mem_v[0].reshape(bs, n_mem, xw))
    kv_shape = (2, DSA_KV_HEADS, DSA_DIM)
    return (y_p.reshape(bp, tp, d), y_s.reshape(bs, ts, d),
            ckv_p.reshape(1, bp, tp, -1), kr_p.reshape(1, bp, tp, -1), kv_p.reshape((1, bp, tp) + kv_shape),
            ik_p.reshape(1, bp, tp, -1), mk_p.reshape(1, bp, n_mem, X_HEADS, X_DIM),
            mv_p.reshape(1, bp, n_mem, X_HEADS, X_DIM),
            ckv_s.reshape(1, bs, ts, -1), kr_s.reshape(1, bs, ts, -1), kv_s.reshape((1, bs, ts) + kv_shape),
            ik_s.reshape(1, bs, ts, -1))
```
